```python
import math
import jax
import jax.numpy as jnp
from jax import lax
import numpy as np

D_MODEL = 1024
BATCH = 8
SEQ = 2048
DEPTH = 2

GRID_W = 64
CTX_LEN = 256

DN_HEADS = 4
DN_HEAD_DIM = D_MODEL // 8
DN_WIDTH = DN_HEADS * DN_HEAD_DIM
POOL_WINDOWS = (2, 4, 8, 16)
POOL_GROUPS = len(POOL_WINDOWS)
POOL_WIDTH = D_MODEL // 4
POOL_GROUP_DIM = POOL_WIDTH // POOL_GROUPS
FNET_GROUPS = 4
FNET_WIDTH = D_MODEL - DN_WIDTH - POOL_WIDTH
FNET_GROUP_DIM = FNET_WIDTH // FNET_GROUPS
CONV_WIDTH = 4
CHUNK = 64

COL_QKV = 3 * DN_WIDTH
COL_BETA = COL_QKV + 2 * DN_HEADS
COL_ALPHA = COL_BETA + 2 * DN_HEADS
COL_Z = COL_ALPHA + DN_WIDTH
COL_POOL = COL_Z + POOL_WIDTH
COL_FNET = COL_POOL + FNET_WIDTH
IN_WIDTH = COL_FNET

PEER_HEADS = 8
N_KEYS = 128
N_EXPERTS = N_KEYS * N_KEYS
PEER_TOPK = 16
PEER_KEY_DIM = 256
PEER_KEY_HALF = PEER_KEY_DIM // 2
PEER_BLOCK = 128

N_MOD = 6
RMS_EPS = 1e-6
L2_EPS = 1e-6

kernel_name = 'hybrid_deltanet_pool_fnet_peer_dit'


def _rmsnorm(x, w):
    xf = x.astype(jnp.float32)
    y = xf * lax.rsqrt(jnp.mean(xf * xf, axis=-1, keepdims=True) + RMS_EPS)
    return (y * w.astype(jnp.float32)).astype(x.dtype)


def _l2norm(x):
    return x * lax.rsqrt(jnp.sum(x * x, axis=-1, keepdims=True) + L2_EPS)


def _modulate(h, shift, scale):
    return h * (1 + scale) + shift


def _short_conv(x, w):
    left = CONV_WIDTH // 2
    right = CONV_WIDTH - 1 - left
    return lax.conv_general_dilated(x, w[:, None, :].astype(x.dtype), window_strides=(1,),
                                    padding=[(left, right)],
                                    dimension_numbers=('NWC', 'WIO', 'NWC'),
                                    feature_group_count=x.shape[-1])


def _delta_prep(p_qkv, p_beta, p_alpha, conv_w, a_log, dt_bias):
    b, l, _ = p_qkv.shape
    qkv = jax.nn.silu(_short_conv(p_qkv, conv_w)).astype(jnp.float32)
    qkv = qkv.reshape(b, l, 3, DN_HEADS, DN_HEAD_DIM).transpose(2, 0, 3, 1, 4)
    q = _l2norm(qkv[0]) * (DN_HEAD_DIM ** -0.5)
    k = _l2norm(qkv[1])
    v = qkv[2]
    beta = jax.nn.sigmoid(p_beta.astype(jnp.float32)).reshape(b, l, 2, DN_HEADS).transpose(2, 0, 3, 1)
    a = p_alpha.astype(jnp.float32).reshape(b, l, 2, DN_HEADS).transpose(2, 0, 3, 1)
    g = -jnp.exp(a_log.astype(jnp.float32))[:, None, :, None] * jax.nn.softplus(
        a + dt_bias.astype(jnp.float32)[:, None, :, None])
    return q, k, v, g, beta


def _gated_delta_chunked(q, k, v, g, beta, s0):
    b, h, l, _ = q.shape
    dv = v.shape[-1]
    n = l // CHUNK
    q, k, v = (t.reshape(b, h, n, CHUNK, t.shape[-1]) for t in (q, k, v))
    g = jnp.cumsum(g.reshape(b, h, n, CHUNK), axis=-1)
    beta = beta.reshape(b, h, n, CHUNK)
    incl = jnp.tril(jnp.ones((CHUNK, CHUNK), dtype=bool))
    strict = jnp.tril(jnp.ones((CHUNK, CHUNK), dtype=bool), -1)
    decay = jnp.exp(jnp.where(incl, g[..., :, None] - g[..., None, :], -jnp.inf))
    k_beta = k * beta[..., None]
    lower = jnp.where(strict, jnp.einsum('bhnid,bhnjd->bhnij', k_beta, k) * decay, 0.0)
    a_mat = jnp.eye(CHUNK, dtype=jnp.float32) + lower
    rhs = jnp.concatenate([v * beta[..., None], k_beta * jnp.exp(g)[..., None]], axis=-1)
    sol = lax.linalg.triangular_solve(a_mat, rhs, left_side=True, lower=True, unit_diagonal=True)
    u, w = sol[..., :dv], sol[..., dv:]
    intra = jnp.where(incl, jnp.einsum('bhnid,bhnjd->bhnij', q, k) * decay, 0.0)
    q_decay = q * jnp.exp(g)[..., None]
    k_tail = k * jnp.exp(g[..., -1:] - g)[..., None]
    chunk_decay = jnp.exp(g[..., -1])
    xs = tuple(jnp.moveaxis(t, 2, 0) for t in (q_decay, k_tail, u, w, intra, chunk_decay))

    def step(s, inp):
        qd, kt, uc, wc, ac, dc = inp
        v_new = uc - jnp.einsum('bhcd,bhde->bhce', wc, s)
        o = jnp.einsum('bhcd,bhde->bhce', qd, s) + jnp.einsum('bhcj,bhje->bhce', ac, v_new)
        s = s * dc[..., None, None] + jnp.einsum('bhcd,bhce->bhde', kt, v_new)
        return s, o

    s_fin, o = lax.scan(step, s0, xs)
    return jnp.moveaxis(o, 0, 2).reshape(b, h, l, dv), s_fin


def _rev(t, on):
    return jnp.flip(t, axis=2) if on else t


def _bidir_delta(lat, ctx):
    ql, kl, vl, gl, bl = lat
    qc, kc, vc, gc, bc = ctx
    s0 = jnp.zeros((ql.shape[0], DN_HEADS, DN_HEAD_DIM, DN_HEAD_DIM), jnp.float32)
    o_lat = 0.0
    o_ctx = 0.0
    for d in range(2):
        rv = d == 1
        oc, sc = _gated_delta_chunked(_rev(qc, rv), _rev(kc, rv), _rev(vc, rv),
                                      _rev(gc[d], rv), _rev(bc[d], rv), s0)
        ol, _ = _gated_delta_chunked(_rev(ql, rv), _rev(kl, rv), _rev(vl, rv),
                                     _rev(gl[d], rv), _rev(bl[d], rv), sc)
        o_lat = o_lat + _rev(ol, rv)
        o_ctx = o_ctx + _rev(oc, rv)
    return o_lat, o_ctx


def _dn_output(o, zg, dn_norm_w):
    b, h, l, dh = o.shape
    o = o.transpose(0, 2, 1, 3)
    o = o * lax.rsqrt(jnp.mean(o * o, axis=-1, keepdims=True) + RMS_EPS) * dn_norm_w.astype(jnp.float32)
    o = o * jax.nn.silu(zg.astype(jnp.float32)).reshape(b, l, h, dh)
    return o.reshape(b, l, DN_WIDTH).astype(zg.dtype)


def _centred_window_mean(x, window):
    s_len = x.shape[2]
    cs = jnp.cumsum(x, axis=2)
    cs = jnp.concatenate([jnp.zeros_like(cs[:, :, :1]), cs], axis=2)
    pos = jnp.arange(s_len)
    lo = jnp.clip(pos - window // 2, 0, s_len)
    hi = jnp.clip(pos + window - window // 2, 0, s_len)
    total = jnp.take(cs, hi, axis=2) - jnp.take(cs, lo, axis=2)
    return total / (hi - lo).astype(x.dtype)[:, None]


def _pool_mix(p, w_pool, pool_scale, rows):
    b, l, _ = p.shape
    pg = p.astype(jnp.float32).reshape(b, rows, l // rows, POOL_GROUPS, POOL_GROUP_DIM)
    y = jnp.stack([_centred_window_mean(pg[..., i, :], w) - pg[..., i, :]
                   for i, w in enumerate(POOL_WINDOWS)], axis=-2)
    y = jnp.einsum('brsgc,gcd->brsgd', y, w_pool.astype(jnp.float32))
    return (y.reshape(b, l, POOL_WIDTH) * pool_scale.astype(jnp.float32)).astype(p.dtype)


def _fourier_mix(p, w_fnet):
    b, l, _ = p.shape
    pg = p.astype(jnp.float32).reshape(b, l, FNET_GROUPS, FNET_GROUP_DIM)
    y = jnp.fft.fft2(pg, axes=(1, 3), norm='ortho').real
    y = jnp.einsum('blgc,gcd->blgd', y, w_fnet.astype(jnp.float32))
    return y.reshape(b, l, FNET_WIDTH).astype(p.dtype)


def _mixer_out(p, o_dn, rows, dn_norm_w, w_pool, pool_scale, w_fnet, w_out):
    dn = _dn_output(o_dn, p[..., COL_ALPHA:COL_Z], dn_norm_w)
    pool = _pool_mix(p[..., COL_Z:COL_POOL], w_pool, pool_scale, rows)
    fnet = _fourier_mix(p[..., COL_POOL:COL_FNET], w_fnet)
    return jnp.concatenate([dn, pool, fnet], axis=-1) @ w_out


def _peer(h, w_query, sub_keys, expert_u, expert_v):
    t, d = h.shape
    q = (h @ w_query).astype(jnp.float32).reshape(t, PEER_HEADS, 2, PEER_KEY_HALF)
    scores = jnp.einsum('thpc,hpnc->thpn', q, sub_keys.astype(jnp.float32))
    s_top, i_top = lax.top_k(scores, PEER_TOPK)
    cand = s_top[:, :, 0, :, None] + s_top[:, :, 1, None, :]
    cand_idx = i_top[:, :, 0, :, None] * N_KEYS + i_top[:, :, 1, None, :]
    best, pos = lax.top_k(cand.reshape(t, PEER_HEADS, PEER_TOPK * PEER_TOPK), PEER_TOPK)
    idx = jnp.take_along_axis(cand_idx.reshape(t, PEER_HEADS, PEER_TOPK * PEER_TOPK), pos, axis=-1)
    gate = jax.nn.softmax(best, axis=-1)
    n_blk = t // PEER_BLOCK

    def retrieve(args):
        hb, ib, gb = args
        act = jax.nn.gelu(jnp.einsum('td,thkd->thk', hb, expert_u[ib]).astype(jnp.float32),
                          approximate=False)
        wgt = (gb * act).astype(hb.dtype)
        return jnp.einsum('thk,thkd->td', wgt, expert_v[ib])

    out = lax.map(retrieve, (h.reshape(n_blk, PEER_BLOCK, d),
                             idx.reshape(n_blk, PEER_BLOCK, PEER_HEADS, PEER_TOPK),
                             gate.reshape(n_blk, PEER_BLOCK, PEER_HEADS, PEER_TOPK)))
    return out.reshape(t, d)


def _layer(x, z, c, c_ctx, rows, update_ctx, w_mod, b_mod, norm1_w, norm2_w, w_in, conv_w,
           a_log, dt_bias, dn_norm_w, w_pool, pool_scale, w_fnet, w_out, w_query, sub_keys,
           expert_u, expert_v):
    b, l, d = x.shape
    m = z.shape[1]
    xs1, xc1, xg1, xs2, xc2, xg2 = jnp.split((jax.nn.silu(c) @ w_mod + b_mod)[:, None, :], N_MOD, axis=-1)
    zs1, zc1, zg1, zs2, zc2, zg2 = jnp.split(jax.nn.silu(c_ctx) @ w_mod + b_mod, N_MOD, axis=-1)

    px = _modulate(_rmsnorm(x, norm1_w), xs1, xc1) @ w_in
    pz = _modulate(_rmsnorm(z, norm1_w), zs1, zc1) @ (w_in if update_ctx else w_in[:, :COL_ALPHA])
    lat = _delta_prep(px[..., :COL_QKV], px[..., COL_QKV:COL_BETA], px[..., COL_BETA:COL_ALPHA],
                      conv_w, a_log, dt_bias)
    ctx_in = _delta_prep(pz[..., :COL_QKV], pz[..., COL_QKV:COL_BETA], pz[..., COL_BETA:COL_ALPHA],
                         conv_w, a_log, dt_bias)
    o_lat, o_ctx = _bidir_delta(lat, ctx_in)
    x = x + xg1 * _mixer_out(px, o_lat, rows, dn_norm_w, w_pool, pool_scale, w_fnet, w_out)
    if update_ctx:
        z = z + zg1 * _mixer_out(pz, o_ctx, 1, dn_norm_w, w_pool, pool_scale, w_fnet, w_out)

    hx = _modulate(_rmsnorm(x, norm2_w), xs2, xc2)
    x = x + xg2 * _peer(hx.reshape(b * l, d), w_query, sub_keys, expert_u, expert_v).reshape(b, l, d)
    if update_ctx:
        hz = _modulate(_rmsnorm(z, norm2_w), zs2, zc2)
        z = z + zg2 * _peer(hz.reshape(b * m, d), w_query, sub_keys, expert_u, expert_v).reshape(b, m, d)
    return x, z


def setup_inputs(seed: int = 0) -> dict:
    key = jax.random.key(seed)
    ks = jax.random.split(key, 24)
    d = D_MODEL

    def nrm(k, shape, scale):
        return jax.random.normal(k, shape, jnp.float32) * scale

    dt = jnp.exp(jax.random.uniform(ks[9], (DEPTH, 2, DN_HEADS), jnp.float32,
                                    minval=math.log(1e-3), maxval=math.log(1e-1)))
    return {
        'x': nrm(ks[0], (BATCH, SEQ, d), 1.0),
        'c': nrm(ks[1], (BATCH, d), 1.0),
        'ctx': nrm(ks[2], (BATCH, CTX_LEN, d), 1.0),
        'c_ctx': nrm(ks[3], (d,), 1.0),
        'w_mod': nrm(ks[4], (DEPTH, d, N_MOD * d), 0.5 * d ** -0.5),
        'b_mod': nrm(ks[5], (DEPTH, N_MOD * d), 0.02),
        'norm1_w': 1.0 + nrm(ks[6], (DEPTH, d), 0.02),
        'norm2_w': 1.0 + nrm(ks[7], (DEPTH, d), 0.02),
        'w_in': nrm(ks[8], (DEPTH, d, IN_WIDTH), d ** -0.5),
        'conv_w': nrm(ks[10], (DEPTH, CONV_WIDTH, COL_QKV), CONV_WIDTH ** -0.5),
        'a_log': jnp.log(jax.random.uniform(ks[11], (DEPTH, 2, DN_HEADS), jnp.float32, minval=1.0, maxval=16.0)),
        'dt_bias': dt + jnp.log(-jnp.expm1(-dt)),
        'dn_norm_w': 1.0 + nrm(ks[12], (DEPTH, DN_HEAD_DIM), 0.02),
        'w_pool': nrm(ks[13], (DEPTH, POOL_GROUPS, POOL_GROUP_DIM, POOL_GROUP_DIM), POOL_GROUP_DIM ** -0.5),
        'pool_scale': 1.0 + nrm(ks[14], (DEPTH, POOL_WIDTH), 0.1),
        'w_fnet': nrm(ks[15], (DEPTH, FNET_GROUPS, FNET_GROUP_DIM, FNET_GROUP_DIM), FNET_GROUP_DIM ** -0.5),
        'w_out': nrm(ks[16], (DEPTH, d, d), d ** -0.5),
        'w_query': nrm(ks[17], (DEPTH, d, PEER_HEADS * PEER_KEY_DIM), d ** -0.5),
        'sub_keys': nrm(ks[18], (DEPTH, PEER_HEADS, 2, N_KEYS, PEER_KEY_HALF), PEER_KEY_HALF ** -0.5),
        'expert_u': nrm(ks[19], (DEPTH, N_EXPERTS, d), d ** -0.5),
        'expert_v': nrm(ks[20], (DEPTH, N_EXPERTS, d), PEER_HEADS ** -0.5),
        'final_norm_w': 1.0 + nrm(ks[21], (d,), 0.02),
    }


def reference(x, c, ctx, c_ctx, w_mod, b_mod, norm1_w, norm2_w, w_in, conv_w, a_log, dt_bias,
              dn_norm_w, w_pool, pool_scale, w_fnet, w_out, w_query, sub_keys, expert_u, expert_v,
              final_norm_w):
    rows = x.shape[1] // GRID_W
    z = ctx
    for i in range(DEPTH):
        x, z = _layer(x, z, c, c_ctx, rows, i < DEPTH - 1, w_mod[i], b_mod[i], norm1_w[i], norm2_w[i],
                      w_in[i], conv_w[i], a_log[i], dt_bias[i], dn_norm_w[i], w_pool[i], pool_scale[i],
                      w_fnet[i], w_out[i], w_query[i], sub_keys[i], expert_u[i], expert_v[i])
    return _rmsnorm(x, final_norm_w)
```

```python
import functools
import math

import numpy as np
import jax
import jax.numpy as jnp
from jax import lax
from jax.experimental import pallas as pl
from jax.experimental.pallas import tpu as pltpu

F32 = jnp.float32
BF16 = jnp.bfloat16
HIGHEST = lax.Precision.HIGHEST

D_MODEL = 1024
DN_HEADS = 4
DN_HEAD_DIM = 128
DN_WIDTH = DN_HEADS * DN_HEAD_DIM
POOL_WINDOWS = (2, 4, 8, 16)
POOL_WIDTH = 256
POOL_GROUP_DIM = 64
FNET_WIDTH = 256
FNET_GROUP_DIM = 64
CONV_WIDTH = 4
CHUNK = 64
GRID_W = 64
PEER_HEADS = 8
N_KEYS = 128
N_EXPERTS = N_KEYS * N_KEYS
PEER_TOPK = 16
N_MOD = 6
RMS_EPS = 1e-6
L2_EPS = 1e-6

COL_Q, COL_K, COL_V, COL_Z = 0, 512, 1024, 1536
COL_POOL, COL_FNET, COL_BA = 2048, 2304, 2560
IN_PAD = 2688
REF_QKV, REF_BETA, REF_ALPHA, REF_Z, REF_POOL, REF_FNET = 1536, 1544, 1552, 2064, 2320, 2576

LANES = 128
VMEM_LIMIT = 56 * 1024 * 1024


def _params(sem):
    return pltpu.CompilerParams(dimension_semantics=sem, vmem_limit_bytes=VMEM_LIMIT)


def _mm(a, b):
    return jnp.dot(a.astype(BF16), b.astype(BF16), preferred_element_type=F32)


def _mm_nt(a, b):
    return lax.dot_general(a.astype(BF16), b.astype(BF16), (((1,), (1,)), ((), ())),
                           preferred_element_type=F32)


def _mm_f32(a, b):
    return jnp.dot(a, b, precision=HIGHEST, preferred_element_type=F32)


def _silu(x):
    return x * jax.nn.sigmoid(x)


def _mod_kernel(c_ref, w_ref, b_ref, o_ref):
    o_ref[0] = _mm_f32(_silu(c_ref[...]), w_ref[0]) + b_ref[0]


def _modulation(cs, w_mod, b_mod):
    depth, d, n = w_mod.shape
    tn = 1536
    return pl.pallas_call(
        _mod_kernel,
        grid=(depth, n // tn),
        in_specs=[pl.BlockSpec((16, d), lambda i, j: (0, 0)),
                  pl.BlockSpec((1, d, tn), lambda i, j: (i, 0, j)),
                  pl.BlockSpec((1, 1, tn), lambda i, j: (i, 0, j))],
        out_specs=pl.BlockSpec((1, 16, tn), lambda i, j: (i, 0, j)),
        out_shape=jax.ShapeDtypeStruct((depth, 16, n), F32),
        compiler_params=_params(("parallel", "parallel")),
        name="modulation",
    )(cs, w_mod, b_mod.reshape(depth, 1, n))


def _norm_mod(x, nw, shift, scale):
    y = x * lax.rsqrt(jnp.mean(x * x, axis=-1, keepdims=True) + RMS_EPS) * nw
    return y * (1.0 + scale) + shift


def _inproj_kernel(x_ref, mod_ref, nw_ref, w_ref, o_ref):
    h = _norm_mod(x_ref[0], nw_ref[...], mod_ref[0, 0:1, :], mod_ref[0, 1:2, :])
    o_ref[0] = _mm(h, w_ref[...])


def _inproj(x, mods, nw, w):
    b, l, d = x.shape
    n = w.shape[1]
    tm = min(512, l)
    return pl.pallas_call(
        _inproj_kernel,
        grid=(b, l // tm),
        in_specs=[pl.BlockSpec((1, tm, d), lambda i, j: (i, j, 0)),
                  pl.BlockSpec((1, 8, d), lambda i, j: (i, 0, 0)),
                  pl.BlockSpec((1, d), lambda i, j: (0, 0)),
                  pl.BlockSpec((d, n), lambda i, j: (0, 0))],
        out_specs=pl.BlockSpec((1, tm, n), lambda i, j: (i, j, 0)),
        out_shape=jax.ShapeDtypeStruct((b, l, n), F32),
        compiler_params=_params(("parallel", "parallel")),
        name="inproj",
    )(x, mods, nw, w)


def _delta_kernel(qz_ref, kz_ref, vz_ref, zz_ref, qx_ref, kx_ref, vx_ref, zx_ref,
                  gate_ref, cw_ref, par_ref, dnw_ref, ox_ref, oz_ref,
                  q_s, k_s, v_s, gf_s, gb_s, bf_s, bb_s, o_s, pad_s, *, m_len, l_len):
    ltot = m_len + l_len
    n_ctx = m_len // CHUNK
    n_tot = ltot // CHUNK
    off = CHUNK

    def conv_silu(src_ref, w, length):
        pad_s[off - 8:off, :] = jnp.zeros((8, LANES), F32)
        pad_s[off:off + length, :] = src_ref[0]
        pad_s[off + length:off + length + 8, :] = jnp.zeros((8, LANES), F32)
        y = (w[0:1, :] * pad_s[off - 2:off - 2 + length, :]
             + w[1:2, :] * pad_s[off - 1:off - 1 + length, :]
             + w[2:3, :] * pad_s[off:off + length, :]
             + w[3:4, :] * pad_s[off + 1:off + 1 + length, :])
        return _silu(y)

    def l2norm(t):
        return t * lax.rsqrt(jnp.sum(t * t, axis=-1, keepdims=True) + L2_EPS)

    for (src_q, src_k, src_v, start, length) in ((qz_ref, kz_ref, vz_ref, 0, m_len),
                                                 (qx_ref, kx_ref, vx_ref, m_len, l_len)):
        q_s[start:start + length, :] = l2norm(conv_silu(src_q, cw_ref[0, 0], length)) * (DN_HEAD_DIM ** -0.5)
        k_s[start:start + length, :] = l2norm(conv_silu(src_k, cw_ref[0, 1], length))
        v_s[start:start + length, :] = conv_silu(src_v, cw_ref[0, 2], length)

    gt = gate_ref[0, 0]
    par = par_ref[0]
    shape = (ltot, LANES)
    bf_s[...] = jnp.broadcast_to(jax.nn.sigmoid(gt[:, 0:1]), shape)
    bb_s[...] = jnp.broadcast_to(jax.nn.sigmoid(gt[:, 1:2]), shape)

    def softplus(t):
        return jnp.maximum(t, 0.0) + jnp.log1p(jnp.exp(-jnp.abs(t)))

    gf_s[...] = jnp.broadcast_to(-jnp.exp(par[0:1, 0:1]) * softplus(gt[:, 2:3] + par[2:3, 0:1]), shape)
    gb_s[...] = jnp.broadcast_to(-jnp.exp(par[1:2, 0:1]) * softplus(gt[:, 3:4] + par[3:4, 0:1]), shape)

    pos = lax.broadcasted_iota(jnp.int32, shape, 0) & (CHUNK - 1)
    pad_s[0:off, :] = jnp.zeros((off, LANES), F32)
    pad_s[off + ltot:off + ltot + off, :] = jnp.zeros((off, LANES), F32)
    step = 1
    while step < CHUNK:
        pad_s[off:off + ltot, :] = gf_s[...]
        gf_s[...] = gf_s[...] + jnp.where(pos >= step, pad_s[off - step:off - step + ltot, :], 0.0)
        pad_s[off:off + ltot, :] = gb_s[...]
        gb_s[...] = gb_s[...] + jnp.where(pos < CHUNK - step, pad_s[off + step:off + step + ltot, :], 0.0)
        step *= 2

    ri = lax.broadcasted_iota(jnp.int32, (CHUNK, CHUNK), 0)
    ci = lax.broadcasted_iota(jnp.int32, (CHUNK, CHUNK), 1)
    eye = (ri == ci).astype(F32)

    for direction in (0, 1):
        fwd = direction == 0
        incl = (ri >= ci) if fwd else (ri <= ci)
        strict = (ri > ci) if fwd else (ri < ci)
        g_s = gf_s if fwd else gb_s
        b_s = bf_s if fwd else bb_s

        def body(t, state, fwd=fwd, incl=incl, strict=strict, g_s=g_s, b_s=b_s):
            if fwd:
                n = t
            else:
                n = jnp.where(t < n_ctx, n_ctx - 1 - t, n_tot + n_ctx - 1 - t)
            sl = pl.ds(pl.multiple_of(n * CHUNK, CHUNK), CHUNK)
            q = q_s[sl, :]
            k = k_s[sl, :]
            v = v_s[sl, :]
            gi = g_s[sl, :]
            bi = b_s[sl, :]
            gj = gi.T[0:CHUNK, :]
            diff = gi[:, 0:CHUNK] - gj
            decay = jnp.where(incl, jnp.exp(jnp.where(incl, diff, 0.0)), 0.0)
            kb = k * bi
            nil = -jnp.where(strict, _mm_nt(kb, k) * decay, 0.0)
            inv = eye + nil
            pw = nil
            for _ in range(5):
                pw = _mm_f32(pw, pw)
                inv = inv + _mm_f32(inv, pw)
            eg = jnp.exp(gi)
            sol = _mm_f32(inv, jnp.concatenate([v * bi, kb * eg], axis=1))
            u = sol[:, 0:DN_HEAD_DIM]
            w = sol[:, DN_HEAD_DIM:]
            intra = jnp.where(incl, _mm_nt(q, k) * decay, 0.0)
            g_last = gi[CHUNK - 1:CHUNK, :] if fwd else gi[0:1, :]
            k_tail = k * jnp.exp(g_last - gi)
            v_new = u - _mm(w, state)
            o = _mm(q * eg, state) + _mm(intra, v_new)
            state = state * jnp.exp(g_last) + _mm(k_tail.T, v_new)
            if fwd:
                o_s[sl, :] = o
            else:
                o_s[sl, :] = o_s[sl, :] + o
            return state

        lax.fori_loop(0, n_tot, body, jnp.zeros((DN_HEAD_DIM, DN_HEAD_DIM), F32))

    o = o_s[...]
    o = o * lax.rsqrt(jnp.mean(o * o, axis=-1, keepdims=True) + RMS_EPS) * dnw_ref[...]
    oz_ref[0] = o[0:m_len, :] * _silu(zz_ref[0])
    ox_ref[0] = o[m_len:, :] * _silu(zx_ref[0])


def _delta(pz, px, gates, conv_w, par, dn_norm_w):
    b, m_len, _ = pz.shape
    l_len = px.shape[1]
    ltot = m_len + l_len
    hb = LANES

    def col(c0, length):
        return pl.BlockSpec((1, length, hb), lambda i, h: (i, 0, c0 // hb + h))

    scratch = [pltpu.VMEM((ltot, LANES), F32) for _ in range(8)]
    scratch.append(pltpu.VMEM((ltot + 2 * CHUNK, LANES), F32))
    return pl.pallas_call(
        functools.partial(_delta_kernel, m_len=m_len, l_len=l_len),
        grid=(b, DN_HEADS),
        in_specs=[col(COL_Q, m_len), col(COL_K, m_len), col(COL_V, m_len), col(COL_Z, m_len),
                  col(COL_Q, l_len), col(COL_K, l_len), col(COL_V, l_len), col(COL_Z, l_len),
                  pl.BlockSpec((1, 1, ltot, 4), lambda i, h: (i, h, 0, 0)),
                  pl.BlockSpec((1, 3, CONV_WIDTH, hb), lambda i, h: (h, 0, 0, 0)),
                  pl.BlockSpec((1, 8, hb), lambda i, h: (h, 0, 0)),
                  pl.BlockSpec((1, hb), lambda i, h: (0, 0))],
        out_specs=[pl.BlockSpec((1, l_len, hb), lambda i, h: (i, 0, h)),
                   pl.BlockSpec((1, m_len, hb), lambda i, h: (i, 0, h))],
        out_shape=[jax.ShapeDtypeStruct((b, l_len, DN_WIDTH), F32),
                   jax.ShapeDtypeStruct((b, m_len, DN_WIDTH), F32)],
        scratch_shapes=scratch,
        compiler_params=_params(("parallel", "parallel")),
        name="delta",
    )(pz, pz, pz, pz, px, px, px, px, gates, conv_w, par, dn_norm_w)


def _fnet_weight_kernel(c_ref, s_ref, w_ref, o_ref, *, scale):
    w = w_ref[...]
    o_ref[:, 0:FNET_WIDTH] = _mm_f32(c_ref[...], w) * scale
    o_ref[:, FNET_WIDTH:] = _mm_f32(s_ref[...], w) * scale


def _fnet_weights(w_bd, seq_len):
    n = FNET_GROUP_DIM
    ang = 2.0 * np.pi * ((np.arange(n)[:, None] * np.arange(n)[None, :]) % n) / n
    eye4 = np.eye(FNET_WIDTH // n)
    c_bd = jnp.asarray(np.kron(eye4, np.cos(ang)), F32)
    s_bd = jnp.asarray(np.kron(eye4, np.sin(ang)), F32)
    return pl.pallas_call(
        functools.partial(_fnet_weight_kernel, scale=float(1.0 / math.sqrt(seq_len * n))),
        out_shape=jax.ShapeDtypeStruct((FNET_WIDTH, 2 * FNET_WIDTH), F32),
        name="fnet_weights",
    )(c_bd, s_bd, w_bd)


def _fnet_kernel(x_ref, w_ref, c_ref, s_ref, o_ref):
    xw = _mm(x_ref[0], w_ref[...])
    o_ref[0] = _mm(c_ref[...], xw[:, 0:FNET_WIDTH]) + _mm(s_ref[...], xw[:, FNET_WIDTH:])


def _dft_mats(seq_len):
    idx = (np.arange(seq_len)[:, None] * np.arange(seq_len)[None, :]) % seq_len
    ang = 2.0 * np.pi * idx / seq_len
    return jnp.asarray(np.cos(ang), F32).astype(BF16), jnp.asarray(-np.sin(ang), F32).astype(BF16)


def _fnet(p, wcs):
    b, l, _ = p.shape
    tm = min(512, l)
    cm, sm = _dft_mats(l)
    return pl.pallas_call(
        _fnet_kernel,
        grid=(l // tm, b),
        in_specs=[pl.BlockSpec((1, l, FNET_WIDTH), lambda i, j: (j, 0, COL_FNET // FNET_WIDTH)),
                  pl.BlockSpec((FNET_WIDTH, 2 * FNET_WIDTH), lambda i, j: (0, 0)),
                  pl.BlockSpec((tm, l), lambda i, j: (i, 0)),
                  pl.BlockSpec((tm, l), lambda i, j: (i, 0))],
        out_specs=pl.BlockSpec((1, tm, FNET_WIDTH), lambda i, j: (j, i, 0)),
        out_shape=jax.ShapeDtypeStruct((b, l, FNET_WIDTH), F32),
        compiler_params=_params(("parallel", "parallel")),
        name="fnet",
    )(p, wcs, cm, sm)


def _mixout_kernel(x_ref, dn_ref, p_ref, f_ref, band_ref, icnt_ref, wp_ref, ps_ref, wo_ref,
                   mod_ref, o_ref):
    xp = p_ref[0]
    group = lax.broadcasted_iota(jnp.int32, xp.shape, 1) // POOL_GROUP_DIM
    win_sum = jnp.zeros(xp.shape, F32)
    for g in range(len(POOL_WINDOWS)):
        win_sum = win_sum + _mm(band_ref[g], jnp.where(group == g, xp, 0.0))
    y = win_sum * icnt_ref[...] - xp
    pool = _mm(y, wp_ref[...]) * ps_ref[...]
    out = (_mm(dn_ref[0], wo_ref[0:DN_WIDTH, :])
           + _mm(pool, wo_ref[DN_WIDTH:DN_WIDTH + POOL_WIDTH, :])
           + _mm(f_ref[0], wo_ref[DN_WIDTH + POOL_WIDTH:, :]))
    o_ref[0] = x_ref[0] + mod_ref[0, 2:3, :] * out


def _pool_consts(tm, seg):
    pos = np.arange(tm) % seg
    seg_id = np.arange(tm) // seg
    band = np.zeros((len(POOL_WINDOWS), tm, tm), np.float32)
    icnt = np.zeros((tm, POOL_WIDTH), np.float32)
    for g, w in enumerate(POOL_WINDOWS):
        lo = np.clip(pos - w // 2, 0, seg)
        hi = np.clip(pos + w - w // 2, 0, seg)
        j = np.arange(tm)
        inside = (seg_id[:, None] == seg_id[None, :]) & (pos[None, :] >= lo[:, None]) & (pos[None, :] < hi[:, None])
        band[g] = inside.astype(np.float32)
        icnt[:, g * POOL_GROUP_DIM:(g + 1) * POOL_GROUP_DIM] = (1.0 / (hi - lo))[:, None]
    return jnp.asarray(band).astype(BF16), jnp.asarray(icnt)


def _mixout(x, dn, p, fn, seg, wp_bd, pool_scale, w_out, mods):
    b, l, d = x.shape
    tm = min(256, l)
    band, icnt = _pool_consts(tm, min(seg, tm))
    if seg > tm:
        raise ValueError("pooling segment longer than the token tile")
    return pl.pallas_call(
        _mixout_kernel,
        grid=(b, l // tm),
        in_specs=[pl.BlockSpec((1, tm, d), lambda i, j: (i, j, 0)),
                  pl.BlockSpec((1, tm, DN_WIDTH), lambda i, j: (i, j, 0)),
                  pl.BlockSpec((1, tm, POOL_WIDTH), lambda i, j: (i, j, COL_POOL // POOL_WIDTH)),
                  pl.BlockSpec((1, tm, FNET_WIDTH), lambda i, j: (i, j, 0)),
                  pl.BlockSpec((len(POOL_WINDOWS), tm, tm), lambda i, j: (0, 0, 0)),
                  pl.BlockSpec((tm, POOL_WIDTH), lambda i, j: (0, 0)),
                  pl.BlockSpec((POOL_WIDTH, POOL_WIDTH), lambda i, j: (0, 0)),
                  pl.BlockSpec((1, POOL_WIDTH), lambda i, j: (0, 0)),
                  pl.BlockSpec((d, d), lambda i, j: (0, 0)),
                  pl.BlockSpec((1, 8, d), lambda i, j: (i, 0, 0))],
        out_specs=pl.BlockSpec((1, tm, d), lambda i, j: (i, j, 0)),
        out_shape=jax.ShapeDtypeStruct((b, l, d), F32),
        compiler_params=_params(("parallel", "parallel")),
        name="mixout",
    )(x, dn, p, fn, band, icnt, wp_bd, pool_scale, w_out, mods)


def _peer_kernel(x_ref, mod_ref, nw_ref, wq_ref, keys_ref, u_ref, vt_ref, o_ref,
                 ht_s, s1_s, e1_s, s2_s, e2_s, thr_s, acc_s, *, n_chunks, blocks_per_chunk):
    j = pl.program_id(2)
    neg_inf = float("-inf")

    @pl.when(j == 0)
    def _route():
        h = _norm_mod(x_ref[0], nw_ref[...], mod_ref[0, 3:4, :], mod_ref[0, 4:5, :])
        ht_s[...] = h.T.astype(BF16)

        def top_values(s):
            vals = []
            for _ in range(PEER_TOPK):
                mx = jnp.max(s, axis=0, keepdims=True)
                vals.append(mx)
                s = jnp.where(s == mx, neg_inf, s)
            return vals

        for hd in range(PEER_HEADS):
            qt = jnp.dot(wq_ref[hd * 2 * N_KEYS:(hd + 1) * 2 * N_KEYS, :], ht_s[...],
                         preferred_element_type=F32)
            s1 = _mm(keys_ref[hd, 0], qt[0:N_KEYS, :])
            s2 = _mm(keys_ref[hd, 1], qt[N_KEYS:, :])
            t1 = top_values(s1)
            t2 = top_values(s2)
            cands = [t1[a] + t2[b] for a in range(PEER_TOPK) for b in range(PEER_TOPK)
                     if (a + 1) * (b + 1) <= PEER_TOPK]
            n_pad = (-len(cands)) % 8
            cand = jnp.concatenate(cands + [jnp.full_like(cands[0], neg_inf)] * n_pad, axis=0)
            work = cand
            thr = jnp.full_like(cands[0], neg_inf)
            found = jnp.zeros_like(cands[0])
            for _ in range(PEER_TOPK):
                mx = jnp.max(work, axis=0, keepdims=True)
                cnt = jnp.sum(jnp.where(cand >= mx, 1.0, 0.0), axis=0, keepdims=True)
                hit = jnp.where(cnt >= PEER_TOPK, 1.0 - found, 0.0)
                thr = jnp.where(hit > 0.0, mx, thr)
                found = jnp.maximum(found, hit)
                work = jnp.where(work == mx, neg_inf, work)
            top = t1[0] + t2[0]
            z = jnp.sum(jnp.where(cand >= thr, jnp.exp(cand - top), 0.0), axis=0, keepdims=True)
            s1_s[hd] = s1
            e1_s[hd] = jnp.exp(s1 - t1[0]) / z
            s2_s[hd] = s2
            e2_s[hd] = jnp.exp(s2 - t2[0])
            thr_s[hd:hd + 1, :] = thr
        acc_s[...] = jnp.zeros(acc_s.shape, F32)

    act_in = jnp.dot(u_ref[...], ht_s[...], preferred_element_type=F32)
    parts = []
    for ii in range(blocks_per_chunk):
        i = j * blocks_per_chunk + ii
        a = act_in[ii * N_KEYS:(ii + 1) * N_KEYS, :]
        gate = jnp.zeros(a.shape, F32)
        for hd in range(PEER_HEADS):
            s1row = s1_s[hd, pl.ds(i, 1), :]
            e1row = e1_s[hd, pl.ds(i, 1), :]
            sel = (s1row + s2_s[hd]) >= thr_s[hd:hd + 1, :]
            gate = gate + jnp.where(sel, e2_s[hd], 0.0) * e1row
        gelu = 0.5 * a * (1.0 + lax.erf(a * (2.0 ** -0.5)))
        parts.append((gate * gelu).astype(BF16))
    wgt = jnp.concatenate(parts, axis=0)
    acc_s[...] = acc_s[...] + jnp.dot(vt_ref[...], wgt, preferred_element_type=F32)

    @pl.when(j == n_chunks - 1)
    def _finish():
        o_ref[0] = x_ref[0] + mod_ref[0, 5:6, :] * acc_s[...].T


def _peer(x, mods, nw, wq_t, keys, u, v_t, tm=256, chunk=1024):
    b, l, d = x.shape
    tm = min(tm, l)
    n_chunks = N_EXPERTS // chunk
    kern = functools.partial(_peer_kernel, n_chunks=n_chunks, blocks_per_chunk=chunk // N_KEYS)
    head_shape = (PEER_HEADS, N_KEYS, tm)
    return pl.pallas_call(
        kern,
        grid=(b, l // tm, n_chunks),
        in_specs=[pl.BlockSpec((1, tm, d), lambda i, t, j: (i, t, 0)),
                  pl.BlockSpec((1, 8, d), lambda i, t, j: (i, 0, 0)),
                  pl.BlockSpec((1, d), lambda i, t, j: (0, 0)),
                  pl.BlockSpec(wq_t.shape, lambda i, t, j: (0, 0)),
                  pl.BlockSpec(keys.shape, lambda i, t, j: (0, 0, 0, 0)),
                  pl.BlockSpec((chunk, d), lambda i, t, j: (j, 0)),
                  pl.BlockSpec((d, chunk), lambda i, t, j: (0, j))],
        out_specs=pl.BlockSpec((1, tm, d), lambda i, t, j: (i, t, 0)),
        out_shape=jax.ShapeDtypeStruct((b, l, d), F32),
        scratch_shapes=[pltpu.VMEM((d, tm), BF16),
                        pltpu.VMEM(head_shape, F32), pltpu.VMEM(head_shape, F32),
                        pltpu.VMEM(head_shape, F32), pltpu.VMEM(head_shape, F32),
                        pltpu.VMEM((PEER_HEADS, tm), F32),
                        pltpu.VMEM((d, tm), F32)],
        compiler_params=_params(("parallel", "parallel", "arbitrary")),
        name="peer",
    )(x, mods, nw, wq_t, keys, u, v_t)


def _final_norm_kernel(x_ref, w_ref, o_ref):
    x = x_ref[0]
    o_ref[0] = x * lax.rsqrt(jnp.mean(x * x, axis=-1, keepdims=True) + RMS_EPS) * w_ref[...]


def _final_norm(x, w):
    b, l, d = x.shape
    tm = min(512, l)
    return pl.pallas_call(
        _final_norm_kernel,
        grid=(b, l // tm),
        in_specs=[pl.BlockSpec((1, tm, d), lambda i, j: (i, j, 0)),
                  pl.BlockSpec((1, d), lambda i, j: (0, 0))],
        out_specs=pl.BlockSpec((1, tm, d), lambda i, j: (i, j, 0)),
        out_shape=jax.ShapeDtypeStruct((b, l, d), F32),
        compiler_params=_params(("parallel", "parallel")),
        name="final_norm",
    )(x, w)


def _block_diag(w):
    g, n, _ = w.shape
    out = jnp.zeros((g * n, g * n), w.dtype)
    for i in range(g):
        out = out.at[i * n:(i + 1) * n, i * n:(i + 1) * n].set(w[i])
    return out


def _reorder_w_in(w):
    d = w.shape[0]
    return jnp.concatenate([w[:, :REF_QKV], w[:, REF_ALPHA:REF_Z], w[:, REF_Z:REF_POOL],
                            w[:, REF_POOL:REF_FNET], w[:, REF_QKV:REF_ALPHA],
                            jnp.zeros((d, IN_PAD - REF_FNET), w.dtype)], axis=1).astype(BF16)


def _gate_columns(pz, px):
    ba = jnp.concatenate([pz[:, :, COL_BA:COL_BA + 16], px[:, :, COL_BA:COL_BA + 16]], axis=1)
    b, ltot, _ = ba.shape
    return ba.reshape(b, ltot, 4, DN_HEADS).transpose(0, 3, 1, 2)


def _mod_rows(mod, rows):
    d = mod.shape[1] // N_MOD
    m = mod.reshape(16, N_MOD, d)[rows]
    return jnp.concatenate([m, jnp.zeros((m.shape[0], 8 - N_MOD, d), m.dtype)], axis=1)


def kernel(x, c, ctx, c_ctx, w_mod, b_mod, norm1_w, norm2_w, w_in, conv_w, a_log, dt_bias,
           dn_norm_w, w_pool, pool_scale, w_fnet, w_out, w_query, sub_keys, expert_u, expert_v,
           final_norm_w):
    b, l, d = x.shape
    m = ctx.shape[1]
    depth = w_mod.shape[0]
    rows = l // GRID_W

    cs = jnp.concatenate([c, c_ctx[None, :], jnp.zeros((16 - b - 1, d), F32)], axis=0)
    mod_all = _modulation(cs, w_mod, b_mod)

    z = ctx
    for i in range(depth):
        update_ctx = i < depth - 1
        mods_x = _mod_rows(mod_all[i], jnp.arange(b))
        mods_z = _mod_rows(mod_all[i], jnp.full((b,), b))
        w_in_r = _reorder_w_in(w_in[i])
        nw1 = norm1_w[i][None, :]
        nw2 = norm2_w[i][None, :]

        px = _inproj(x, mods_x, nw1, w_in_r)
        pz = _inproj(z, mods_z, nw1, w_in_r)

        cw = conv_w[i].reshape(CONV_WIDTH, 3, DN_HEADS, DN_HEAD_DIM).transpose(2, 1, 0, 3)
        par = jnp.concatenate([a_log[i], dt_bias[i]], axis=0)
        par = jnp.broadcast_to(par.T[:, :, None], (DN_HEADS, 4, LANES))
        par = jnp.concatenate([par, jnp.zeros((DN_HEADS, 4, LANES), F32)], axis=1)
        dn_x, dn_z = _delta(pz, px, _gate_columns(pz, px), cw, par, dn_norm_w[i][None, :])

        wp_bd = _block_diag(w_pool[i]).astype(BF16)
        wf_bd = _block_diag(w_fnet[i])
        w_out_b = w_out[i].astype(BF16)
        ps = pool_scale[i][None, :]

        fn_x = _fnet(px, _fnet_weights(wf_bd, l).astype(BF16))
        x = _mixout(x, dn_x, px, fn_x, l // rows, wp_bd, ps, w_out_b, mods_x)
        if update_ctx:
            fn_z = _fnet(pz, _fnet_weights(wf_bd, m).astype(BF16))
            z = _mixout(z, dn_z, pz, fn_z, m, wp_bd, ps, w_out_b, mods_z)

        wq_t = w_query[i].T.astype(BF16)
        keys = sub_keys[i].astype(BF16)
        u_b = expert_u[i].astype(BF16)
        v_t = expert_v[i].T.astype(BF16)
        x = _peer(x, mods_x, nw2, wq_t, keys, u_b, v_t)
        if update_ctx:
            z = _peer(z, mods_z, nw2, wq_t, keys, u_b, v_t)

    return _final_norm(x, final_norm_w[None, :])
```

```python
import functools
import math

import numpy as np
import jax
import jax.numpy as jnp
from jax import lax
from jax.experimental import pallas as pl
from jax.experimental.pallas import tpu as pltpu

F32 = jnp.float32
BF16 = jnp.bfloat16
HIGHEST = lax.Precision.HIGHEST

D_MODEL = 1024
DN_HEADS = 4
DN_HEAD_DIM = 128
DN_WIDTH = DN_HEADS * DN_HEAD_DIM
POOL_WINDOWS = (2, 4, 8, 16)
POOL_WIDTH = 256
POOL_GROUP_DIM = 64
FNET_WIDTH = 256
FNET_GROUP_DIM = 64
CONV_WIDTH = 4
CHUNK = 64
GRID_W = 64
PEER_HEADS = 8
N_KEYS = 128
N_EXPERTS = N_KEYS * N_KEYS
PEER_TOPK = 16
N_MOD = 6
RMS_EPS = 1e-6
L2_EPS = 1e-6

COL_Q, COL_K, COL_V, COL_Z = 0, 512, 1024, 1536
COL_POOL, COL_FNET, COL_BA = 2048, 2304, 2560
IN_PAD = 2688
REF_QKV, REF_BETA, REF_ALPHA, REF_Z, REF_POOL, REF_FNET = 1536, 1544, 1552, 2064, 2320, 2576

LANES = 128
VMEM_LIMIT = 56 * 1024 * 1024


def _params(sem):
    return pltpu.CompilerParams(dimension_semantics=sem, vmem_limit_bytes=VMEM_LIMIT)


def _mm(a, b):
    return jnp.dot(a.astype(BF16), b.astype(BF16), preferred_element_type=F32)


def _mm_nt(a, b):
    return lax.dot_general(a.astype(BF16), b.astype(BF16), (((1,), (1,)), ((), ())),
                           preferred_element_type=F32)


def _mm_inv(a, b):
    return _mm(a, b)


def _mm_f32(a, b):
    return jnp.dot(a, b, precision=HIGHEST, preferred_element_type=F32)


def _silu(x):
    return x * jax.nn.sigmoid(x)


def _mod_kernel(c_ref, w_ref, b_ref, o_ref):
    o_ref[0] = _mm_f32(_silu(c_ref[...]), w_ref[0]) + b_ref[0]


def _modulation(cs, w_mod, b_mod):
    depth, d, n = w_mod.shape
    tn = 1536
    return pl.pallas_call(
        _mod_kernel,
        grid=(depth, n // tn),
        in_specs=[pl.BlockSpec((16, d), lambda i, j: (0, 0)),
                  pl.BlockSpec((1, d, tn), lambda i, j: (i, 0, j)),
                  pl.BlockSpec((1, 1, tn), lambda i, j: (i, 0, j))],
        out_specs=pl.BlockSpec((1, 16, tn), lambda i, j: (i, 0, j)),
        out_shape=jax.ShapeDtypeStruct((depth, 16, n), F32),
        compiler_params=_params(("parallel", "parallel")),
        name="modulation",
    )(cs, w_mod, b_mod.reshape(depth, 1, n))


def _norm_mod(x, nw, shift, scale):
    y = x * lax.rsqrt(jnp.mean(x * x, axis=-1, keepdims=True) + RMS_EPS) * nw
    return y * (1.0 + scale) + shift


def _inproj_kernel(x_ref, mod_ref, nw_ref, w_ref, o_ref):
    h = _norm_mod(x_ref[0], nw_ref[...], mod_ref[0, 0:1, :], mod_ref[0, 1:2, :])
    o_ref[0] = _mm(h, w_ref[...])


def _inproj(x, mods, nw, w):
    b, l, d = x.shape
    n = w.shape[1]
    tm = min(512, l)
    return pl.pallas_call(
        _inproj_kernel,
        grid=(b, l // tm),
        in_specs=[pl.BlockSpec((1, tm, d), lambda i, j: (i, j, 0)),
                  pl.BlockSpec((1, 8, d), lambda i, j: (i, 0, 0)),
                  pl.BlockSpec((1, d), lambda i, j: (0, 0)),
                  pl.BlockSpec((d, n), lambda i, j: (0, 0))],
        out_specs=pl.BlockSpec((1, tm, n), lambda i, j: (i, j, 0)),
        out_shape=jax.ShapeDtypeStruct((b, l, n), F32),
        compiler_params=_params(("parallel", "parallel")),
        name="inproj",
    )(x, mods, nw, w)


def _delta_kernel(qz_ref, kz_ref, vz_ref, zz_ref, qx_ref, kx_ref, vx_ref, zx_ref,
                  gate_ref, cw_ref, par_ref, dnw_ref, ox_ref, oz_ref,
                  q_s, k_s, v_s, gf_s, gb_s, bf_s, bb_s, o_s, pad_s, kw_s, c_s, qe_s, o0_s, dc_s,
                  *, m_len, l_len):
    ltot = m_len + l_len
    n_ctx = m_len // CHUNK
    n_tot = ltot // CHUNK
    off = CHUNK

    def conv_silu(src_ref, w, length):
        pad_s[off - 8:off, :] = jnp.zeros((8, LANES), F32)
        pad_s[off:off + length, :] = src_ref[0]
        pad_s[off + length:off + length + 8, :] = jnp.zeros((8, LANES), F32)
        y = (w[0:1, :] * pad_s[off - 2:off - 2 + length, :]
             + w[1:2, :] * pad_s[off - 1:off - 1 + length, :]
             + w[2:3, :] * pad_s[off:off + length, :]
             + w[3:4, :] * pad_s[off + 1:off + 1 + length, :])
        return _silu(y)

    def l2norm(t):
        return t * lax.rsqrt(jnp.sum(t * t, axis=-1, keepdims=True) + L2_EPS)

    for (src_q, src_k, src_v, start, length) in ((qz_ref, kz_ref, vz_ref, 0, m_len),
                                                 (qx_ref, kx_ref, vx_ref, m_len, l_len)):
        q_s[start:start + length, :] = l2norm(conv_silu(src_q, cw_ref[0, 0], length)) * (DN_HEAD_DIM ** -0.5)
        k_s[start:start + length, :] = l2norm(conv_silu(src_k, cw_ref[0, 1], length))
        v_s[start:start + length, :] = conv_silu(src_v, cw_ref[0, 2], length)

    gt = gate_ref[0, 0]
    par = par_ref[0]
    shape = (ltot, LANES)
    bf_s[...] = jnp.broadcast_to(jax.nn.sigmoid(gt[:, 0:1]), shape)
    bb_s[...] = jnp.broadcast_to(jax.nn.sigmoid(gt[:, 1:2]), shape)

    def softplus(t):
        return jnp.maximum(t, 0.0) + jnp.log1p(jnp.exp(-jnp.abs(t)))

    gf_s[...] = jnp.broadcast_to(-jnp.exp(par[0:1, 0:1]) * softplus(gt[:, 2:3] + par[2:3, 0:1]), shape)
    gb_s[...] = jnp.broadcast_to(-jnp.exp(par[1:2, 0:1]) * softplus(gt[:, 3:4] + par[3:4, 0:1]), shape)

    pos = lax.broadcasted_iota(jnp.int32, shape, 0) & (CHUNK - 1)
    pad_s[0:off, :] = jnp.zeros((off, LANES), F32)
    pad_s[off + ltot:off + ltot + off, :] = jnp.zeros((off, LANES), F32)
    step = 1
    while step < CHUNK:
        pad_s[off:off + ltot, :] = gf_s[...]
        gf_s[...] = gf_s[...] + jnp.where(pos >= step, pad_s[off - step:off - step + ltot, :], 0.0)
        pad_s[off:off + ltot, :] = gb_s[...]
        gb_s[...] = gb_s[...] + jnp.where(pos < CHUNK - step, pad_s[off + step:off + step + ltot, :], 0.0)
        step *= 2

    ri = lax.broadcasted_iota(jnp.int32, (CHUNK, CHUNK), 0)
    ci = lax.broadcasted_iota(jnp.int32, (CHUNK, CHUNK), 1)
    eye = (ri == ci).astype(F32)
    hd = DN_HEAD_DIM
    pair_mask = {1 << b: jnp.logical_and((ri >> (b + 1)) == (ci >> (b + 1)), (ri >> b) != (ci >> b))
                 for b in range(6)}

    group = 4 if n_tot % 4 == 0 else 2

    def stage1(g, carry):
        chains = [(g * group + c, d) for c in range(group) for d in (0, 1)]
        both = lambda fn: [fn(n, d) for (n, d) in chains]
        each = lambda fn, *cols: [fn(*vals) for vals in zip(*cols)]
        rows = lambda n: pl.ds(pl.multiple_of(n * CHUNK, CHUNK), CHUNK)
        incl = both(lambda n, d: (ri >= ci) if d == 0 else (ri <= ci))
        strict = both(lambda n, d: (ri > ci) if d == 0 else (ri < ci))
        q = both(lambda n, d: q_s[rows(n), :])
        k = both(lambda n, d: k_s[rows(n), :])
        v = both(lambda n, d: v_s[rows(n), :])
        gi = both(lambda n, d: (gf_s if d == 0 else gb_s)[rows(n), :])
        bi = both(lambda n, d: (bf_s if d == 0 else bb_s)[rows(n), :])
        diff = each(lambda g_: g_[:, 0:CHUNK] - g_.T[0:CHUNK, :], gi)
        decay = each(lambda m, df: jnp.where(m, jnp.exp(jnp.where(m, df, 0.0)), 0.0), incl, diff)
        kb = each(lambda k_, b_: k_ * b_, k, bi)
        kk = each(_mm_nt, kb, k)
        low = each(lambda m, kk_, dec: jnp.where(m, kk_ * dec, 0.0), strict, kk, decay)
        inv = [eye - jnp.where(pair_mask[1], l_, 0.0) for l_ in low]
        size = 2
        while size < CHUNK:
            coupling = [jnp.where(pair_mask[size], l_, 0.0) for l_ in low]
            left = each(lambda a_, b_: _mm_inv(a_, b_), inv, coupling)
            corr = each(lambda a_, b_: _mm_inv(a_, b_), left, inv)
            inv = each(lambda t, u_: t - u_, inv, corr)
            size *= 2
        eg = [jnp.exp(g_) for g_ in gi]
        rhs = each(lambda kb_, e_, v_, b_: jnp.concatenate([kb_ * e_, v_ * b_], axis=1), kb, eg, v, bi)
        sol = each(_mm, inv, rhs)
        qk = each(_mm_nt, q, k)
        intra = each(lambda m, qk_, dec: jnp.where(m, qk_ * dec, 0.0), incl, qk, decay)
        g_last = [g_[CHUNK - 1:CHUNK, :] if d == 0 else g_[0:1, :] for g_, (_, d) in zip(gi, chains)]
        k_tail_t = each(lambda k_, gl, g_: (k_ * jnp.exp(gl - g_)).T, k, g_last, gi)
        iwu = each(_mm, intra, sol)
        kwu = each(_mm, k_tail_t, sol)
        for idx, (n, d) in enumerate(chains):
            sk = pl.ds(pl.multiple_of(n * hd, hd), hd)
            qe_s[d, rows(n), :] = (q[idx] * eg[idx] - iwu[idx][:, 0:hd]).astype(BF16)
            o0_s[d, rows(n), :] = iwu[idx][:, hd:]
            kw_s[d, sk, :] = kwu[idx][:, 0:hd].astype(BF16)
            c_s[d, sk, :] = kwu[idx][:, hd:]
            dc_s[d, pl.ds(pl.multiple_of(n * 8, 8), 8), :] = jnp.broadcast_to(jnp.exp(g_last[idx]), (8, LANES))
        return carry

    lax.fori_loop(0, n_tot // group, stage1, 0)

    def advance(direction, n, state):
        sl = pl.ds(pl.multiple_of(n * CHUNK, CHUNK), CHUNK)
        sk = pl.ds(pl.multiple_of(n * hd, hd), hd)
        sb = state.astype(BF16)
        o_s[direction, sl, :] = (jnp.dot(qe_s[direction, sl, :], sb, preferred_element_type=F32)
                                 + o0_s[direction, sl, :])
        dc = dc_s[direction, pl.ds(pl.multiple_of(n * 8, 8), 1), :]
        return (state * dc - jnp.dot(kw_s[direction, sk, :], sb, preferred_element_type=F32)
                + c_s[direction, sk, :])

    def stage2(t, states):
        n_b = jnp.where(t < n_ctx, n_ctx - 1 - t, n_tot + n_ctx - 1 - t)
        return advance(0, t, states[0]), advance(1, n_b, states[1])

    zero = jnp.zeros((hd, hd), F32)
    lax.fori_loop(0, n_tot, stage2, (zero, zero))

    o = o_s[0] + o_s[1]
    o = o * lax.rsqrt(jnp.mean(o * o, axis=-1, keepdims=True) + RMS_EPS) * dnw_ref[...]
    oz_ref[0] = o[0:m_len, :] * _silu(zz_ref[0])
    ox_ref[0] = o[m_len:, :] * _silu(zx_ref[0])


def _delta(pz, px, gates, conv_w, par, dn_norm_w):
    b, m_len, _ = pz.shape
    l_len = px.shape[1]
    ltot = m_len + l_len
    n_tot = ltot // CHUNK
    hb = LANES

    def col(c0, length):
        return pl.BlockSpec((1, length, hb), lambda i, h: (i, 0, c0 // hb + h))

    scratch = [pltpu.VMEM((ltot, LANES), F32) for _ in range(7)]
    scratch += [pltpu.VMEM((2, ltot, LANES), F32),
                pltpu.VMEM((ltot + 2 * CHUNK, LANES), F32),
                pltpu.VMEM((2, n_tot * DN_HEAD_DIM, LANES), BF16),
                pltpu.VMEM((2, n_tot * DN_HEAD_DIM, LANES), F32),
                pltpu.VMEM((2, ltot, LANES), BF16),
                pltpu.VMEM((2, ltot, LANES), F32),
                pltpu.VMEM((2, n_tot * 8, LANES), F32)]
    return pl.pallas_call(
        functools.partial(_delta_kernel, m_len=m_len, l_len=l_len),
        grid=(b, DN_HEADS),
        in_specs=[col(COL_Q, m_len), col(COL_K, m_len), col(COL_V, m_len), col(COL_Z, m_len),
                  col(COL_Q, l_len), col(COL_K, l_len), col(COL_V, l_len), col(COL_Z, l_len),
                  pl.BlockSpec((1, 1, ltot, 4), lambda i, h: (i, h, 0, 0)),
                  pl.BlockSpec((1, 3, CONV_WIDTH, hb), lambda i, h: (h, 0, 0, 0)),
                  pl.BlockSpec((1, 8, hb), lambda i, h: (h, 0, 0)),
                  pl.BlockSpec((1, hb), lambda i, h: (0, 0))],
        out_specs=[pl.BlockSpec((1, l_len, hb), lambda i, h: (i, 0, h)),
                   pl.BlockSpec((1, m_len, hb), lambda i, h: (i, 0, h))],
        out_shape=[jax.ShapeDtypeStruct((b, l_len, DN_WIDTH), F32),
                   jax.ShapeDtypeStruct((b, m_len, DN_WIDTH), F32)],
        scratch_shapes=scratch,
        compiler_params=_params(("parallel", "parallel")),
        name="delta",
    )(pz, pz, pz, pz, px, px, px, px, gates, conv_w, par, dn_norm_w)


def _fnet_weight_kernel(c_ref, s_ref, w_ref, o_ref, *, scale):
    w = w_ref[...]
    o_ref[:, 0:FNET_WIDTH] = _mm_f32(c_ref[...], w) * scale
    o_ref[:, FNET_WIDTH:] = _mm_f32(s_ref[...], w) * scale


def _fnet_weights(w_bd, seq_len):
    n = FNET_GROUP_DIM
    ang = 2.0 * np.pi * ((np.arange(n)[:, None] * np.arange(n)[None, :]) % n) / n
    eye4 = np.eye(FNET_WIDTH // n)
    c_bd = jnp.asarray(np.kron(eye4, np.cos(ang)), F32)
    s_bd = jnp.asarray(np.kron(eye4, np.sin(ang)), F32)
    return pl.pallas_call(
        functools.partial(_fnet_weight_kernel, scale=float(1.0 / math.sqrt(seq_len * n))),
        out_shape=jax.ShapeDtypeStruct((FNET_WIDTH, 2 * FNET_WIDTH), F32),
        name="fnet_weights",
    )(c_bd, s_bd, w_bd)


def _fnet_kernel(x_ref, w_ref, c_ref, s_ref, o_ref):
    xw = _mm(x_ref[0], w_ref[...])
    o_ref[0] = _mm(c_ref[...], xw[:, 0:FNET_WIDTH]) + _mm(s_ref[...], xw[:, FNET_WIDTH:])


def _dft_mats(seq_len):
    idx = (np.arange(seq_len)[:, None] * np.arange(seq_len)[None, :]) % seq_len
    ang = 2.0 * np.pi * idx / seq_len
    return jnp.asarray(np.cos(ang), F32).astype(BF16), jnp.asarray(-np.sin(ang), F32).astype(BF16)


def _fnet(p, wcs):
    b, l, _ = p.shape
    tm = min(512, l)
    cm, sm = _dft_mats(l)
    return pl.pallas_call(
        _fnet_kernel,
        grid=(l // tm, b),
        in_specs=[pl.BlockSpec((1, l, FNET_WIDTH), lambda i, j: (j, 0, COL_FNET // FNET_WIDTH)),
                  pl.BlockSpec((FNET_WIDTH, 2 * FNET_WIDTH), lambda i, j: (0, 0)),
                  pl.BlockSpec((tm, l), lambda i, j: (i, 0)),
                  pl.BlockSpec((tm, l), lambda i, j: (i, 0))],
        out_specs=pl.BlockSpec((1, tm, FNET_WIDTH), lambda i, j: (j, i, 0)),
        out_shape=jax.ShapeDtypeStruct((b, l, FNET_WIDTH), F32),
        compiler_params=_params(("parallel", "parallel")),
        name="fnet",
    )(p, wcs, cm, sm)


def _mixout_kernel(x_ref, dn_ref, p_ref, f_ref, band_ref, icnt_ref, wp_ref, ps_ref, wo_ref,
                   mod_ref, o_ref):
    xp = p_ref[0]
    group = lax.broadcasted_iota(jnp.int32, xp.shape, 1) // POOL_GROUP_DIM
    win_sum = jnp.zeros(xp.shape, F32)
    for g in range(len(POOL_WINDOWS)):
        win_sum = win_sum + _mm(band_ref[g], jnp.where(group == g, xp, 0.0))
    y = win_sum * icnt_ref[...] - xp
    pool = _mm(y, wp_ref[...]) * ps_ref[...]
    out = (_mm(dn_ref[0], wo_ref[0:DN_WIDTH, :])
           + _mm(pool, wo_ref[DN_WIDTH:DN_WIDTH + POOL_WIDTH, :])
           + _mm(f_ref[0], wo_ref[DN_WIDTH + POOL_WIDTH:, :]))
    o_ref[0] = x_ref[0] + mod_ref[0, 2:3, :] * out


def _pool_consts(tm, seg):
    pos = np.arange(tm) % seg
    seg_id = np.arange(tm) // seg
    band = np.zeros((len(POOL_WINDOWS), tm, tm), np.float32)
    icnt = np.zeros((tm, POOL_WIDTH), np.float32)
    for g, w in enumerate(POOL_WINDOWS):
        lo = np.clip(pos - w // 2, 0, seg)
        hi = np.clip(pos + w - w // 2, 0, seg)
        j = np.arange(tm)
        inside = (seg_id[:, None] == seg_id[None, :]) & (pos[None, :] >= lo[:, None]) & (pos[None, :] < hi[:, None])
        band[g] = inside.astype(np.float32)
        icnt[:, g * POOL_GROUP_DIM:(g + 1) * POOL_GROUP_DIM] = (1.0 / (hi - lo))[:, None]
    return jnp.asarray(band).astype(BF16), jnp.asarray(icnt)


def _mixout(x, dn, p, fn, seg, wp_bd, pool_scale, w_out, mods):
    b, l, d = x.shape
    tm = min(256, l)
    band, icnt = _pool_consts(tm, min(seg, tm))
    if seg > tm:
        raise ValueError("pooling segment longer than the token tile")
    return pl.pallas_call(
        _mixout_kernel,
        grid=(b, l // tm),
        in_specs=[pl.BlockSpec((1, tm, d), lambda i, j: (i, j, 0)),
                  pl.BlockSpec((1, tm, DN_WIDTH), lambda i, j: (i, j, 0)),
                  pl.BlockSpec((1, tm, POOL_WIDTH), lambda i, j: (i, j, COL_POOL // POOL_WIDTH)),
                  pl.BlockSpec((1, tm, FNET_WIDTH), lambda i, j: (i, j, 0)),
                  pl.BlockSpec((len(POOL_WINDOWS), tm, tm), lambda i, j: (0, 0, 0)),
                  pl.BlockSpec((tm, POOL_WIDTH), lambda i, j: (0, 0)),
                  pl.BlockSpec((POOL_WIDTH, POOL_WIDTH), lambda i, j: (0, 0)),
                  pl.BlockSpec((1, POOL_WIDTH), lambda i, j: (0, 0)),
                  pl.BlockSpec((d, d), lambda i, j: (0, 0)),
                  pl.BlockSpec((1, 8, d), lambda i, j: (i, 0, 0))],
        out_specs=pl.BlockSpec((1, tm, d), lambda i, j: (i, j, 0)),
        out_shape=jax.ShapeDtypeStruct((b, l, d), F32),
        compiler_params=_params(("parallel", "parallel")),
        name="mixout",
    )(x, dn, p, fn, band, icnt, wp_bd, pool_scale, w_out, mods)


def _peer_kernel(x_ref, mod_ref, nw_ref, wq_ref, keys_ref, u_ref, vt_ref, o_ref,
                 ht_s, c1_s, e1_s, r2_s, e2_s, acc_s, *, n_chunks, blocks_per_chunk):
    j = pl.program_id(2)
    neg_inf = float("-inf")

    @pl.when(j == 0)
    def _route():
        h = _norm_mod(x_ref[0], nw_ref[...], mod_ref[0, 3:4, :], mod_ref[0, 4:5, :])
        ht_s[...] = h.T.astype(BF16)

        def top_values(s, with_rank):
            vals = []
            rank = jnp.full(s.shape, 99.0, F32)
            for r in range(PEER_TOPK):
                mx = jnp.max(s, axis=0, keepdims=True)
                vals.append(mx)
                hit = s == mx
                if with_rank:
                    rank = jnp.where(hit, float(r), rank)
                s = jnp.where(hit, neg_inf, s)
            return vals, rank

        for hd in range(PEER_HEADS):
            qt = jnp.dot(wq_ref[hd * 2 * N_KEYS:(hd + 1) * 2 * N_KEYS, :], ht_s[...],
                         preferred_element_type=F32)
            s1 = _mm(keys_ref[hd, 0], qt[0:N_KEYS, :])
            s2 = _mm(keys_ref[hd, 1], qt[N_KEYS:, :])
            t1, _ = top_values(s1, False)
            t2, rank2 = top_values(s2, True)
            cands = [t1[a] + t2[b] for a in range(PEER_TOPK) for b in range(PEER_TOPK)
                     if (a + 1) * (b + 1) <= PEER_TOPK]
            n_pad = (-len(cands)) % 8
            cand = jnp.concatenate(cands + [jnp.full_like(cands[0], neg_inf)] * n_pad, axis=0)
            work = cand
            thr = jnp.full_like(cands[0], neg_inf)
            found = jnp.zeros_like(cands[0])
            for _ in range(PEER_TOPK):
                mx = jnp.max(work, axis=0, keepdims=True)
                cnt = jnp.sum(jnp.where(cand >= mx, 1.0, 0.0), axis=0, keepdims=True)
                hit = jnp.where(cnt >= PEER_TOPK, 1.0 - found, 0.0)
                thr = jnp.where(hit > 0.0, mx, thr)
                found = jnp.maximum(found, hit)
                work = jnp.where(work == mx, neg_inf, work)
            top = t1[0] + t2[0]
            z = jnp.sum(jnp.where(cand >= thr, jnp.exp(cand - top), 0.0), axis=0, keepdims=True)
            cnt1 = jnp.zeros(s1.shape, F32)
            for r in range(PEER_TOPK):
                cnt1 = cnt1 + jnp.where(s1 + t2[r] >= thr, 1.0, 0.0)
            c1_s[hd] = cnt1
            e1_s[hd] = jnp.exp(s1 - t1[0]) / z
            r2_s[hd] = rank2.astype(BF16)
            e2_s[hd] = jnp.exp(s2 - t2[0]).astype(BF16)
        acc_s[...] = jnp.zeros(acc_s.shape, F32)

    act_in = jnp.dot(u_ref[...], ht_s[...], preferred_element_type=F32)
    parts = []
    for ii in range(blocks_per_chunk):
        i = j * blocks_per_chunk + ii
        a = act_in[ii * N_KEYS:(ii + 1) * N_KEYS, :]
        gate = jnp.zeros(a.shape, BF16)
        for hd in range(PEER_HEADS):
            c1row = c1_s[hd, pl.ds(i, 1), :].astype(BF16)
            e1row = e1_s[hd, pl.ds(i, 1), :].astype(BF16)
            e2 = e2_s[hd]
            gate = gate + jnp.where(r2_s[hd] < c1row, e2, jnp.zeros_like(e2)) * e1row
        gelu = 0.5 * a * (1.0 + lax.erf(a * (2.0 ** -0.5)))
        parts.append(gate * gelu.astype(BF16))
    wgt = jnp.concatenate(parts, axis=0)
    acc_s[...] = acc_s[...] + jnp.dot(vt_ref[...], wgt, preferred_element_type=F32)

    @pl.when(j == n_chunks - 1)
    def _finish():
        o_ref[0] = x_ref[0] + mod_ref[0, 5:6, :] * acc_s[...].T


def _peer(x, mods, nw, wq_t, keys, u, v_t, tm=256, chunk=1024):
    b, l, d = x.shape
    tm = min(tm, l)
    n_chunks = N_EXPERTS // chunk
    kern = functools.partial(_peer_kernel, n_chunks=n_chunks, blocks_per_chunk=chunk // N_KEYS)
    head_shape = (PEER_HEADS, N_KEYS, tm)
    return pl.pallas_call(
        kern,
        grid=(b, l // tm, n_chunks),
        in_specs=[pl.BlockSpec((1, tm, d), lambda i, t, j: (i, t, 0)),
                  pl.BlockSpec((1, 8, d), lambda i, t, j: (i, 0, 0)),
                  pl.BlockSpec((1, d), lambda i, t, j: (0, 0)),
                  pl.BlockSpec(wq_t.shape, lambda i, t, j: (0, 0)),
                  pl.BlockSpec(keys.shape, lambda i, t, j: (0, 0, 0, 0)),
                  pl.BlockSpec((chunk, d), lambda i, t, j: (j, 0)),
                  pl.BlockSpec((d, chunk), lambda i, t, j: (0, j))],
        out_specs=pl.BlockSpec((1, tm, d), lambda i, t, j: (i, t, 0)),
        out_shape=jax.ShapeDtypeStruct((b, l, d), F32),
        scratch_shapes=[pltpu.VMEM((d, tm), BF16),
                        pltpu.VMEM(head_shape, F32), pltpu.VMEM(head_shape, F32),
                        pltpu.VMEM(head_shape, BF16), pltpu.VMEM(head_shape, BF16),
                        pltpu.VMEM((d, tm), F32)],
        compiler_params=_params(("parallel", "parallel", "arbitrary")),
        name="peer",
    )(x, mods, nw, wq_t, keys, u, v_t)


def _final_norm_kernel(x_ref, w_ref, o_ref):
    x = x_ref[0]
    o_ref[0] = x * lax.rsqrt(jnp.mean(x * x, axis=-1, keepdims=True) + RMS_EPS) * w_ref[...]


def _final_norm(x, w):
    b, l, d = x.shape
    tm = min(512, l)
    return pl.pallas_call(
        _final_norm_kernel,
        grid=(b, l // tm),
        in_specs=[pl.BlockSpec((1, tm, d), lambda i, j: (i, j, 0)),
                  pl.BlockSpec((1, d), lambda i, j: (0, 0))],
        out_specs=pl.BlockSpec((1, tm, d), lambda i, j: (i, j, 0)),
        out_shape=jax.ShapeDtypeStruct((b, l, d), F32),
        compiler_params=_params(("parallel", "parallel")),
        name="final_norm",
    )(x, w)


def _block_diag(w):
    g, n, _ = w.shape
    out = jnp.zeros((g * n, g * n), w.dtype)
    for i in range(g):
        out = out.at[i * n:(i + 1) * n, i * n:(i + 1) * n].set(w[i])
    return out


def _reorder_w_in(w):
    d = w.shape[0]
    return jnp.concatenate([w[:, :REF_QKV], w[:, REF_ALPHA:REF_Z], w[:, REF_Z:REF_POOL],
                            w[:, REF_POOL:REF_FNET], w[:, REF_QKV:REF_ALPHA],
                            jnp.zeros((d, IN_PAD - REF_FNET), w.dtype)], axis=1).astype(BF16)


def _gate_columns(pz, px):
    ba = jnp.concatenate([pz[:, :, COL_BA:COL_BA + 16], px[:, :, COL_BA:COL_BA + 16]], axis=1)
    b, ltot, _ = ba.shape
    return ba.reshape(b, ltot, 4, DN_HEADS).transpose(0, 3, 1, 2)


def _mod_rows(mod, rows):
    d = mod.shape[1] // N_MOD
    m = mod.reshape(16, N_MOD, d)[rows]
    return jnp.concatenate([m, jnp.zeros((m.shape[0], 8 - N_MOD, d), m.dtype)], axis=1)


def kernel(x, c, ctx, c_ctx, w_mod, b_mod, norm1_w, norm2_w, w_in, conv_w, a_log, dt_bias,
           dn_norm_w, w_pool, pool_scale, w_fnet, w_out, w_query, sub_keys, expert_u, expert_v,
           final_norm_w):
    b, l, d = x.shape
    m = ctx.shape[1]
    depth = w_mod.shape[0]
    rows = l // GRID_W

    cs = jnp.concatenate([c, c_ctx[None, :], jnp.zeros((16 - b - 1, d), F32)], axis=0)
    mod_all = _modulation(cs, w_mod, b_mod)

    z = ctx
    for i in range(depth):
        update_ctx = i < depth - 1
        mods_x = _mod_rows(mod_all[i], jnp.arange(b))
        mods_z = _mod_rows(mod_all[i], jnp.full((b,), b))
        w_in_r = _reorder_w_in(w_in[i])
        nw1 = norm1_w[i][None, :]
        nw2 = norm2_w[i][None, :]

        px = _inproj(x, mods_x, nw1, w_in_r)
        pz = _inproj(z, mods_z, nw1, w_in_r)

        cw = conv_w[i].reshape(CONV_WIDTH, 3, DN_HEADS, DN_HEAD_DIM).transpose(2, 1, 0, 3)
        par = jnp.concatenate([a_log[i], dt_bias[i]], axis=0)
        par = jnp.broadcast_to(par.T[:, :, None], (DN_HEADS, 4, LANES))
        par = jnp.concatenate([par, jnp.zeros((DN_HEADS, 4, LANES), F32)], axis=1)
        dn_x, dn_z = _delta(pz, px, _gate_columns(pz, px), cw, par, dn_norm_w[i][None, :])

        wp_bd = _block_diag(w_pool[i]).astype(BF16)
        wf_bd = _block_diag(w_fnet[i])
        w_out_b = w_out[i].astype(BF16)
        ps = pool_scale[i][None, :]

        fn_x = _fnet(px, _fnet_weights(wf_bd, l).astype(BF16))
        x = _mixout(x, dn_x, px, fn_x, l // rows, wp_bd, ps, w_out_b, mods_x)
        if update_ctx:
            fn_z = _fnet(pz, _fnet_weights(wf_bd, m).astype(BF16))
            z = _mixout(z, dn_z, pz, fn_z, m, wp_bd, ps, w_out_b, mods_z)

        wq_t = w_query[i].T.astype(BF16)
        keys = sub_keys[i].astype(BF16)
        u_b = expert_u[i].astype(BF16)
        v_t = expert_v[i].T.astype(BF16)
        x = _peer(x, mods_x, nw2, wq_t, keys, u_b, v_t)
        if update_ctx:
            z = _peer(z, mods_z, nw2, wq_t, keys, u_b, v_t)

    return _final_norm(x, final_norm_w[None, :])
```

```python
import functools
import math

import numpy as np
import jax
import jax.numpy as jnp
from jax import lax
from jax.experimental import pallas as pl
from jax.experimental.pallas import tpu as pltpu

F32 = jnp.float32
BF16 = jnp.bfloat16
HIGHEST = lax.Precision.HIGHEST

D_MODEL = 1024
DN_HEADS = 4
DN_HEAD_DIM = 128
DN_WIDTH = DN_HEADS * DN_HEAD_DIM
POOL_WINDOWS = (2, 4, 8, 16)
POOL_WIDTH = 256
POOL_GROUP_DIM = 64
FNET_WIDTH = 256
FNET_GROUP_DIM = 64
CONV_WIDTH = 4
CHUNK = 64
GRID_W = 64
PEER_HEADS = 8
N_KEYS = 128
N_EXPERTS = N_KEYS * N_KEYS
PEER_TOPK = 16
N_MOD = 6
RMS_EPS = 1e-6
L2_EPS = 1e-6

COL_Q, COL_K, COL_V, COL_Z = 0, 512, 1024, 1536
COL_POOL, COL_FNET, COL_BA = 2048, 2304, 2560
IN_PAD = 2688
REF_QKV, REF_BETA, REF_ALPHA, REF_Z, REF_POOL, REF_FNET = 1536, 1544, 1552, 2064, 2320, 2576

LANES = 128
VMEM_LIMIT = 56 * 1024 * 1024


def _params(sem):
    return pltpu.CompilerParams(dimension_semantics=sem, vmem_limit_bytes=VMEM_LIMIT)


def _mm(a, b):
    return jnp.dot(a.astype(BF16), b.astype(BF16), preferred_element_type=F32)


def _mm_nt(a, b):
    return lax.dot_general(a.astype(BF16), b.astype(BF16), (((1,), (1,)), ((), ())),
                           preferred_element_type=F32)


def _mm_inv(a, b):
    return _mm(a, b)


def _mm_f32(a, b):
    return jnp.dot(a, b, precision=HIGHEST, preferred_element_type=F32)


def _silu(x):
    return x * jax.nn.sigmoid(x)


def _mod_kernel(c_ref, w_ref, b_ref, o_ref):
    o_ref[0] = _mm_f32(_silu(c_ref[...]), w_ref[0]) + b_ref[0]


def _modulation(cs, w_mod, b_mod):
    depth, d, n = w_mod.shape
    tn = 1536
    return pl.pallas_call(
        _mod_kernel,
        grid=(depth, n // tn),
        in_specs=[pl.BlockSpec((16, d), lambda i, j: (0, 0)),
                  pl.BlockSpec((1, d, tn), lambda i, j: (i, 0, j)),
                  pl.BlockSpec((1, 1, tn), lambda i, j: (i, 0, j))],
        out_specs=pl.BlockSpec((1, 16, tn), lambda i, j: (i, 0, j)),
        out_shape=jax.ShapeDtypeStruct((depth, 16, n), F32),
        compiler_params=_params(("parallel", "parallel")),
        name="modulation",
    )(cs, w_mod, b_mod.reshape(depth, 1, n))


def _norm_mod(x, nw, shift, scale):
    y = x * lax.rsqrt(jnp.mean(x * x, axis=-1, keepdims=True) + RMS_EPS) * nw
    return y * (1.0 + scale) + shift


def _inproj_kernel(x_ref, mod_ref, nw_ref, w_ref, o_ref):
    h = _norm_mod(x_ref[0], nw_ref[...], mod_ref[0, 0:1, :], mod_ref[0, 1:2, :])
    o_ref[0] = _mm(h, w_ref[...])


def _inproj(x, mods, nw, w):
    b, l, d = x.shape
    n = w.shape[1]
    tm = min(512, l)
    return pl.pallas_call(
        _inproj_kernel,
        grid=(b, l // tm),
        in_specs=[pl.BlockSpec((1, tm, d), lambda i, j: (i, j, 0)),
                  pl.BlockSpec((1, 8, d), lambda i, j: (i, 0, 0)),
                  pl.BlockSpec((1, d), lambda i, j: (0, 0)),
                  pl.BlockSpec((d, n), lambda i, j: (0, 0))],
        out_specs=pl.BlockSpec((1, tm, n), lambda i, j: (i, j, 0)),
        out_shape=jax.ShapeDtypeStruct((b, l, n), F32),
        compiler_params=_params(("parallel", "parallel")),
        name="inproj",
    )(x, mods, nw, w)


def _delta_kernel(qz_ref, kz_ref, vz_ref, zz_ref, qx_ref, kx_ref, vx_ref, zx_ref,
                  gate_ref, cw_ref, par_ref, dnw_ref, ox_ref, oz_ref,
                  q_s, k_s, v_s, gf_s, gb_s, bf_s, bb_s, o_s, pad_s, kw_s, c_s, qe_s, o0_s, dc_s,
                  *, m_len, l_len):
    ltot = m_len + l_len
    n_ctx = m_len // CHUNK
    n_tot = ltot // CHUNK
    off = CHUNK

    def conv_silu(src_ref, w, length):
        pad_s[off - 8:off, :] = jnp.zeros((8, LANES), F32)
        pad_s[off:off + length, :] = src_ref[0]
        pad_s[off + length:off + length + 8, :] = jnp.zeros((8, LANES), F32)
        y = (w[0:1, :] * pad_s[off - 2:off - 2 + length, :]
             + w[1:2, :] * pad_s[off - 1:off - 1 + length, :]
             + w[2:3, :] * pad_s[off:off + length, :]
             + w[3:4, :] * pad_s[off + 1:off + 1 + length, :])
        return _silu(y)

    def l2norm(t):
        return t * lax.rsqrt(jnp.sum(t * t, axis=-1, keepdims=True) + L2_EPS)

    for (src_q, src_k, src_v, start, length) in ((qz_ref, kz_ref, vz_ref, 0, m_len),
                                                 (qx_ref, kx_ref, vx_ref, m_len, l_len)):
        q_s[start:start + length, :] = l2norm(conv_silu(src_q, cw_ref[0, 0], length)) * (DN_HEAD_DIM ** -0.5)
        k_s[start:start + length, :] = l2norm(conv_silu(src_k, cw_ref[0, 1], length))
        v_s[start:start + length, :] = conv_silu(src_v, cw_ref[0, 2], length)

    gt = gate_ref[0, 0]
    par = par_ref[0]
    shape = (ltot, LANES)
    bf_s[...] = jnp.broadcast_to(jax.nn.sigmoid(gt[:, 0:1]), shape)
    bb_s[...] = jnp.broadcast_to(jax.nn.sigmoid(gt[:, 1:2]), shape)

    def softplus(t):
        return jnp.maximum(t, 0.0) + jnp.log1p(jnp.exp(-jnp.abs(t)))

    gf_s[...] = jnp.broadcast_to(-jnp.exp(par[0:1, 0:1]) * softplus(gt[:, 2:3] + par[2:3, 0:1]), shape)
    gb_s[...] = jnp.broadcast_to(-jnp.exp(par[1:2, 0:1]) * softplus(gt[:, 3:4] + par[3:4, 0:1]), shape)

    pos = lax.broadcasted_iota(jnp.int32, shape, 0) & (CHUNK - 1)
    pad_s[0:off, :] = jnp.zeros((off, LANES), F32)
    pad_s[off + ltot:off + ltot + off, :] = jnp.zeros((off, LANES), F32)
    step = 1
    while step < CHUNK:
        pad_s[off:off + ltot, :] = gf_s[...]
        gf_s[...] = gf_s[...] + jnp.where(pos >= step, pad_s[off - step:off - step + ltot, :], 0.0)
        pad_s[off:off + ltot, :] = gb_s[...]
        gb_s[...] = gb_s[...] + jnp.where(pos < CHUNK - step, pad_s[off + step:off + step + ltot, :], 0.0)
        step *= 2

    ri = lax.broadcasted_iota(jnp.int32, (CHUNK, CHUNK), 0)
    ci = lax.broadcasted_iota(jnp.int32, (CHUNK, CHUNK), 1)
    eye = (ri == ci).astype(F32)
    hd = DN_HEAD_DIM
    pair_mask = {1 << b: jnp.logical_and((ri >> (b + 1)) == (ci >> (b + 1)), (ri >> b) != (ci >> b))
                 for b in range(6)}

    group = 4 if n_tot % 4 == 0 else 2

    def stage1(g, carry):
        chains = [(g * group + c, d) for c in range(group) for d in (0, 1)]
        both = lambda fn: [fn(n, d) for (n, d) in chains]
        each = lambda fn, *cols: [fn(*vals) for vals in zip(*cols)]
        rows = lambda n: pl.ds(pl.multiple_of(n * CHUNK, CHUNK), CHUNK)
        incl = both(lambda n, d: (ri >= ci) if d == 0 else (ri <= ci))
        strict = both(lambda n, d: (ri > ci) if d == 0 else (ri < ci))
        q = both(lambda n, d: q_s[rows(n), :])
        k = both(lambda n, d: k_s[rows(n), :])
        v = both(lambda n, d: v_s[rows(n), :])
        gi = both(lambda n, d: (gf_s if d == 0 else gb_s)[rows(n), :])
        bi = both(lambda n, d: (bf_s if d == 0 else bb_s)[rows(n), :])
        diff = each(lambda g_: g_[:, 0:CHUNK] - g_.T[0:CHUNK, :], gi)
        decay = each(lambda m, df: jnp.where(m, jnp.exp(jnp.where(m, df, 0.0)), 0.0), incl, diff)
        kb = each(lambda k_, b_: k_ * b_, k, bi)
        kk = each(_mm_nt, kb, k)
        low = each(lambda m, kk_, dec: jnp.where(m, kk_ * dec, 0.0), strict, kk, decay)
        inv = [eye - jnp.where(pair_mask[1], l_, 0.0) for l_ in low]
        size = 2
        while size < CHUNK:
            coupling = [jnp.where(pair_mask[size], l_, 0.0) for l_ in low]
            left = each(lambda a_, b_: _mm_inv(a_, b_), inv, coupling)
            corr = each(lambda a_, b_: _mm_inv(a_, b_), left, inv)
            inv = each(lambda t, u_: t - u_, inv, corr)
            size *= 2
        eg = [jnp.exp(g_) for g_ in gi]
        rhs = each(lambda kb_, e_, v_, b_: jnp.concatenate([kb_ * e_, v_ * b_], axis=1), kb, eg, v, bi)
        sol = each(_mm, inv, rhs)
        qk = each(_mm_nt, q, k)
        intra = each(lambda m, qk_, dec: jnp.where(m, qk_ * dec, 0.0), incl, qk, decay)
        g_last = [g_[CHUNK - 1:CHUNK, :] if d == 0 else g_[0:1, :] for g_, (_, d) in zip(gi, chains)]
        k_tail_t = each(lambda k_, gl, g_: (k_ * jnp.exp(gl - g_)).T, k, g_last, gi)
        iwu = each(_mm, intra, sol)
        kwu = each(_mm, k_tail_t, sol)
        for idx, (n, d) in enumerate(chains):
            sk = pl.ds(pl.multiple_of(n * hd, hd), hd)
            qe_s[d, rows(n), :] = (q[idx] * eg[idx] - iwu[idx][:, 0:hd]).astype(BF16)
            o0_s[d, rows(n), :] = iwu[idx][:, hd:]
            kw_s[d, sk, :] = kwu[idx][:, 0:hd].astype(BF16)
            c_s[d, sk, :] = kwu[idx][:, hd:]
            dc_s[d, pl.ds(pl.multiple_of(n * 8, 8), 8), :] = jnp.broadcast_to(jnp.exp(g_last[idx]), (8, LANES))
        return carry

    lax.fori_loop(0, n_tot // group, stage1, 0)

    def advance(direction, n, state):
        sl = pl.ds(pl.multiple_of(n * CHUNK, CHUNK), CHUNK)
        sk = pl.ds(pl.multiple_of(n * hd, hd), hd)
        sb = state.astype(BF16)
        o_s[direction, sl, :] = (jnp.dot(qe_s[direction, sl, :], sb, preferred_element_type=F32)
                                 + o0_s[direction, sl, :])
        dc = dc_s[direction, pl.ds(pl.multiple_of(n * 8, 8), 1), :]
        return (state * dc - jnp.dot(kw_s[direction, sk, :], sb, preferred_element_type=F32)
                + c_s[direction, sk, :])

    def stage2(t, states):
        n_b = jnp.where(t < n_ctx, n_ctx - 1 - t, n_tot + n_ctx - 1 - t)
        return advance(0, t, states[0]), advance(1, n_b, states[1])

    zero = jnp.zeros((hd, hd), F32)
    lax.fori_loop(0, n_tot, stage2, (zero, zero))

    o = o_s[0] + o_s[1]
    o = o * lax.rsqrt(jnp.mean(o * o, axis=-1, keepdims=True) + RMS_EPS) * dnw_ref[...]
    oz_ref[0] = o[0:m_len, :] * _silu(zz_ref[0])
    ox_ref[0] = o[m_len:, :] * _silu(zx_ref[0])


def _delta(pz, px, gates, conv_w, par, dn_norm_w):
    b, m_len, _ = pz.shape
    l_len = px.shape[1]
    ltot = m_len + l_len
    n_tot = ltot // CHUNK
    hb = LANES

    def col(c0, length):
        return pl.BlockSpec((1, length, hb), lambda i, h: (i, 0, c0 // hb + h))

    scratch = [pltpu.VMEM((ltot, LANES), F32) for _ in range(7)]
    scratch += [pltpu.VMEM((2, ltot, LANES), F32),
                pltpu.VMEM((ltot + 2 * CHUNK, LANES), F32),
                pltpu.VMEM((2, n_tot * DN_HEAD_DIM, LANES), BF16),
                pltpu.VMEM((2, n_tot * DN_HEAD_DIM, LANES), F32),
                pltpu.VMEM((2, ltot, LANES), BF16),
                pltpu.VMEM((2, ltot, LANES), F32),
                pltpu.VMEM((2, n_tot * 8, LANES), F32)]
    return pl.pallas_call(
        functools.partial(_delta_kernel, m_len=m_len, l_len=l_len),
        grid=(b, DN_HEADS),
        in_specs=[col(COL_Q, m_len), col(COL_K, m_len), col(COL_V, m_len), col(COL_Z, m_len),
                  col(COL_Q, l_len), col(COL_K, l_len), col(COL_V, l_len), col(COL_Z, l_len),
                  pl.BlockSpec((1, 1, ltot, 4), lambda i, h: (i, h, 0, 0)),
                  pl.BlockSpec((1, 3, CONV_WIDTH, hb), lambda i, h: (h, 0, 0, 0)),
                  pl.BlockSpec((1, 8, hb), lambda i, h: (h, 0, 0)),
                  pl.BlockSpec((1, hb), lambda i, h: (0, 0))],
        out_specs=[pl.BlockSpec((1, l_len, hb), lambda i, h: (i, 0, h)),
                   pl.BlockSpec((1, m_len, hb), lambda i, h: (i, 0, h))],
        out_shape=[jax.ShapeDtypeStruct((b, l_len, DN_WIDTH), F32),
                   jax.ShapeDtypeStruct((b, m_len, DN_WIDTH), F32)],
        scratch_shapes=scratch,
        compiler_params=_params(("parallel", "parallel")),
        name="delta",
    )(pz, pz, pz, pz, px, px, px, px, gates, conv_w, par, dn_norm_w)


def _fnet_weight_kernel(c_ref, s_ref, w_ref, o_ref, *, scale):
    w = w_ref[...]
    o_ref[:, 0:FNET_WIDTH] = _mm_f32(c_ref[...], w) * scale
    o_ref[:, FNET_WIDTH:] = _mm_f32(s_ref[...], w) * scale


def _fnet_weights(w_bd, seq_len):
    n = FNET_GROUP_DIM
    ang = 2.0 * np.pi * ((np.arange(n)[:, None] * np.arange(n)[None, :]) % n) / n
    eye4 = np.eye(FNET_WIDTH // n)
    c_bd = jnp.asarray(np.kron(eye4, np.cos(ang)), F32)
    s_bd = jnp.asarray(np.kron(eye4, np.sin(ang)), F32)
    return pl.pallas_call(
        functools.partial(_fnet_weight_kernel, scale=float(1.0 / math.sqrt(seq_len * n))),
        out_shape=jax.ShapeDtypeStruct((FNET_WIDTH, 2 * FNET_WIDTH), F32),
        name="fnet_weights",
    )(c_bd, s_bd, w_bd)


def _fnet_kernel(x_ref, w_ref, c_ref, s_ref, o_ref):
    xw = _mm(x_ref[0], w_ref[...])
    o_ref[0] = _mm(c_ref[...], xw[:, 0:FNET_WIDTH]) + _mm(s_ref[...], xw[:, FNET_WIDTH:])


def _dft_mats(seq_len):
    idx = (np.arange(seq_len)[:, None] * np.arange(seq_len)[None, :]) % seq_len
    ang = 2.0 * np.pi * idx / seq_len
    return jnp.asarray(np.cos(ang), F32).astype(BF16), jnp.asarray(-np.sin(ang), F32).astype(BF16)


def _fnet(p, wcs):
    b, l, _ = p.shape
    tm = min(512, l)
    cm, sm = _dft_mats(l)
    return pl.pallas_call(
        _fnet_kernel,
        grid=(l // tm, b),
        in_specs=[pl.BlockSpec((1, l, FNET_WIDTH), lambda i, j: (j, 0, COL_FNET // FNET_WIDTH)),
                  pl.BlockSpec((FNET_WIDTH, 2 * FNET_WIDTH), lambda i, j: (0, 0)),
                  pl.BlockSpec((tm, l), lambda i, j: (i, 0)),
                  pl.BlockSpec((tm, l), lambda i, j: (i, 0))],
        out_specs=pl.BlockSpec((1, tm, FNET_WIDTH), lambda i, j: (j, i, 0)),
        out_shape=jax.ShapeDtypeStruct((b, l, FNET_WIDTH), F32),
        compiler_params=_params(("parallel", "parallel")),
        name="fnet",
    )(p, wcs, cm, sm)


def _mixout_kernel(x_ref, dn_ref, p_ref, f_ref, band_ref, icnt_ref, wp_ref, ps_ref, wo_ref,
                   mod_ref, o_ref):
    xp = p_ref[0]
    group = lax.broadcasted_iota(jnp.int32, xp.shape, 1) // POOL_GROUP_DIM
    win_sum = jnp.zeros(xp.shape, F32)
    for g in range(len(POOL_WINDOWS)):
        win_sum = win_sum + _mm(band_ref[g], jnp.where(group == g, xp, 0.0))
    y = win_sum * icnt_ref[...] - xp
    pool = _mm(y, wp_ref[...]) * ps_ref[...]
    out = (_mm(dn_ref[0], wo_ref[0:DN_WIDTH, :])
           + _mm(pool, wo_ref[DN_WIDTH:DN_WIDTH + POOL_WIDTH, :])
           + _mm(f_ref[0], wo_ref[DN_WIDTH + POOL_WIDTH:, :]))
    o_ref[0] = x_ref[0] + mod_ref[0, 2:3, :] * out


def _pool_consts(tm, seg):
    pos = np.arange(tm) % seg
    seg_id = np.arange(tm) // seg
    band = np.zeros((len(POOL_WINDOWS), tm, tm), np.float32)
    icnt = np.zeros((tm, POOL_WIDTH), np.float32)
    for g, w in enumerate(POOL_WINDOWS):
        lo = np.clip(pos - w // 2, 0, seg)
        hi = np.clip(pos + w - w // 2, 0, seg)
        j = np.arange(tm)
        inside = (seg_id[:, None] == seg_id[None, :]) & (pos[None, :] >= lo[:, None]) & (pos[None, :] < hi[:, None])
        band[g] = inside.astype(np.float32)
        icnt[:, g * POOL_GROUP_DIM:(g + 1) * POOL_GROUP_DIM] = (1.0 / (hi - lo))[:, None]
    return jnp.asarray(band).astype(BF16), jnp.asarray(icnt)


def _mixout(x, dn, p, fn, seg, wp_bd, pool_scale, w_out, mods):
    b, l, d = x.shape
    tm = min(256, l)
    band, icnt = _pool_consts(tm, min(seg, tm))
    if seg > tm:
        raise ValueError("pooling segment longer than the token tile")
    return pl.pallas_call(
        _mixout_kernel,
        grid=(b, l // tm),
        in_specs=[pl.BlockSpec((1, tm, d), lambda i, j: (i, j, 0)),
                  pl.BlockSpec((1, tm, DN_WIDTH), lambda i, j: (i, j, 0)),
                  pl.BlockSpec((1, tm, POOL_WIDTH), lambda i, j: (i, j, COL_POOL // POOL_WIDTH)),
                  pl.BlockSpec((1, tm, FNET_WIDTH), lambda i, j: (i, j, 0)),
                  pl.BlockSpec((len(POOL_WINDOWS), tm, tm), lambda i, j: (0, 0, 0)),
                  pl.BlockSpec((tm, POOL_WIDTH), lambda i, j: (0, 0)),
                  pl.BlockSpec((POOL_WIDTH, POOL_WIDTH), lambda i, j: (0, 0)),
                  pl.BlockSpec((1, POOL_WIDTH), lambda i, j: (0, 0)),
                  pl.BlockSpec((d, d), lambda i, j: (0, 0)),
                  pl.BlockSpec((1, 8, d), lambda i, j: (i, 0, 0))],
        out_specs=pl.BlockSpec((1, tm, d), lambda i, j: (i, j, 0)),
        out_shape=jax.ShapeDtypeStruct((b, l, d), F32),
        compiler_params=_params(("parallel", "parallel")),
        name="mixout",
    )(x, dn, p, fn, band, icnt, wp_bd, pool_scale, w_out, mods)


def _peer_kernel(x_ref, mod_ref, nw_ref, wq_ref, keys_ref, u_ref, vt_ref, o_ref,
                 ht_s, c1_s, e1_s, r2_s, e2_s, acc_s, act_s, acta_s, wga_s, wgb_s, sc_s, rc_s, re_s,
                 *, n_chunks, blocks_per_half):
    p = pl.program_id(2)
    neg_inf = float("-inf")
    half = blocks_per_half * N_KEYS
    n_lane_groups = x_ref.shape[1] // LANES
    d_model = x_ref.shape[2]

    @pl.when(p == 0)
    def _route():
        h = _norm_mod(x_ref[0], nw_ref[...], mod_ref[0, 3:4, :], mod_ref[0, 4:5, :])
        ht_s[...] = h.T.astype(BF16)

        marker = 2.0 ** 100

        def extract(s):
            vals = []
            for r in range(PEER_TOPK):
                mx = jnp.max(s, axis=0, keepdims=True)
                vals.append(mx)
                s = jnp.where(s == mx, -marker * (r + 1), s)
            rank = jnp.where(s < -0.5 * marker, s * (-1.0 / marker) - 1.0, 99.0)
            return vals, rank

        pairs = [(a, b) for a in range(PEER_TOPK) for b in range(PEER_TOPK)
                 if (a + 1) * (b + 1) <= PEER_TOPK]
        n_pad = (-len(pairs)) % 8

        for hd in range(PEER_HEADS):
            qt = jnp.dot(wq_ref[hd * 2 * N_KEYS:(hd + 1) * 2 * N_KEYS, :], ht_s[...],
                         preferred_element_type=F32)
            sc_s[0] = _mm(keys_ref[hd, 0], qt[0:N_KEYS, :])
            sc_s[1] = _mm(keys_ref[hd, 1], qt[N_KEYS:, :])
            for tg in range(n_lane_groups):
                ln = slice(tg * LANES, (tg + 1) * LANES)
                s1 = sc_s[0, :, ln]
                s2 = sc_s[1, :, ln]
                t1, rank1 = extract(s1)
                t2, rank2 = extract(s2)
                cands = [t1[a] + t2[b] for a, b in pairs]
                cand = jnp.concatenate(cands + [jnp.full_like(cands[0], neg_inf)] * n_pad, axis=0)
                work = cand
                thr = jnp.full_like(cands[0], neg_inf)
                found = jnp.zeros_like(cands[0])
                for _ in range(PEER_TOPK):
                    mx = jnp.max(work, axis=0, keepdims=True)
                    cnt = jnp.sum(jnp.where(cand >= mx, 1.0, 0.0), axis=0, keepdims=True)
                    hit = jnp.where(cnt >= PEER_TOPK, 1.0 - found, 0.0)
                    thr = jnp.where(hit > 0.0, mx, thr)
                    found = jnp.maximum(found, hit)
                    work = jnp.where(work == mx, neg_inf, work)
                z = jnp.sum(jnp.where(cand >= thr, jnp.exp(cand - cands[0]), 0.0), axis=0, keepdims=True)
                cnt1 = jnp.zeros(s1.shape, F32)
                for a in range(PEER_TOPK):
                    partners = jnp.zeros_like(thr)
                    for idx, (pa, _) in enumerate(pairs):
                        if pa == a:
                            partners = partners + jnp.where(cands[idx] >= thr, 1.0, 0.0)
                    cnt1 = jnp.where(rank1 == float(a), partners, cnt1)
                c1_s[hd, :, ln] = cnt1
                e1_s[hd, :, ln] = jnp.exp(s1 - t1[0]) * (1.0 / z)
                r2_s[hd, :, ln] = rank2.astype(BF16)
                e2_s[hd, :, ln] = jnp.exp(s2 - t2[0]).astype(BF16)
        acc_s[...] = jnp.zeros(acc_s.shape, F32)
        act_s[...] = jnp.zeros(act_s.shape, F32)
        wga_s[...] = jnp.zeros(wga_s.shape, BF16)

    def row_bf16(row):
        return pltpu.repeat(jnp.broadcast_to(row, (16, LANES)).astype(BF16), N_KEYS // 16, axis=0)

    def stage_rows(first_block, slot):
        for ii in range(blocks_per_half):
            i = jnp.maximum(first_block + ii, 0)
            for hd in range(PEER_HEADS):
                k = ii * PEER_HEADS + hd
                rc_s[slot, k:k + 1, :] = c1_s[hd, pl.ds(i, 1), :]
                re_s[slot, k:k + 1, :] = e1_s[hd, pl.ds(i, 1), :]

    def gate_unit(act, slot, out_ref, tg, g0):
        ln = slice(tg * LANES, (tg + 1) * LANES)
        blocks = list(range(g0, min(g0 + 4, blocks_per_half)))
        gates = [jnp.zeros((N_KEYS, LANES), BF16) for _ in blocks]
        for hd in range(PEER_HEADS):
            r2 = r2_s[hd, :, ln]
            e2 = e2_s[hd, :, ln]
            for n, ii in enumerate(blocks):
                k = ii * PEER_HEADS + hd
                c1row = row_bf16(rc_s[slot, k:k + 1, ln])
                e1row = row_bf16(re_s[slot, k:k + 1, ln])
                gates[n] = gates[n] + jnp.where(r2 < c1row, e2, jnp.zeros_like(e2)) * e1row
        for n, ii in enumerate(blocks):
            a = act[ii * N_KEYS:(ii + 1) * N_KEYS, ln]
            gelu = 0.5 * a * (1.0 + lax.erf(a * (2.0 ** -0.5)))
            out_ref[ii * N_KEYS:(ii + 1) * N_KEYS, ln] = gates[n] * gelu.astype(BF16)

    def phase(matmul_piece, act, slot, out_ref, lane_groups):
        units = [(tg, g0) for tg in lane_groups for g0 in range(0, blocks_per_half, 4)]
        for idx, unit in enumerate(units):
            matmul_piece(idx, len(units))
            gate_unit(act, slot, out_ref, *unit)

    def value_piece(lo, w_ref):
        def piece(idx, n_pieces):
            rows = d_model // n_pieces
            r0 = idx * rows
            acc_s[r0:r0 + rows, :] += jnp.dot(vt_ref[r0:r0 + rows, lo:lo + half], w_ref[...],
                                              preferred_element_type=F32)
        return piece

    def act_piece(lo, dst_ref):
        def piece(idx, n_pieces):
            rows = half // n_pieces
            r0 = idx * rows
            dst_ref[r0:r0 + rows, :] = jnp.dot(u_ref[lo + r0:lo + r0 + rows, :], ht_s[...],
                                               preferred_element_type=F32)
        return piece

    prev_b = (2 * p - 1) * blocks_per_half

    first = tuple(range(n_lane_groups // 2))
    second = tuple(range(n_lane_groups // 2, n_lane_groups))

    @pl.when(p < n_chunks)
    def _stage():
        stage_rows(prev_b, 0)
        stage_rows(2 * p * blocks_per_half, 1)

    @pl.when(p + 1 <= n_chunks)
    def _phase1():
        phase(value_piece(0, wga_s), act_s, 0, wgb_s, first)

    @pl.when(p + 2 <= n_chunks + 1)
    def _phase2():
        phase(act_piece(0, acta_s), act_s, 0, wgb_s, second)

    @pl.when(p + 3 <= n_chunks + 2)
    def _phase3():
        phase(value_piece(half, wgb_s), acta_s, 1, wga_s, first)

    @pl.when(p + 4 <= n_chunks + 3)
    def _phase4():
        phase(act_piece(half, act_s), acta_s, 1, wga_s, second)

    @pl.when(p == n_chunks)
    def _drain1():
        stage_rows(prev_b, 0)
        phase(value_piece(0, wga_s), act_s, 0, wgb_s, first + second)

    @pl.when(p >= n_chunks)
    def _drain2():
        out = acc_s[...] + jnp.dot(vt_ref[:, half:], wgb_s[...], preferred_element_type=F32)
        o_ref[0] = x_ref[0] + mod_ref[0, 5:6, :] * out.T


def _peer(x, mods, nw, wq_t, keys, u, v_t, tm=512, chunk=2048):
    b, l, d = x.shape
    tm = min(tm, l)
    n_chunks = N_EXPERTS // chunk
    half = chunk // 2
    kern = functools.partial(_peer_kernel, n_chunks=n_chunks, blocks_per_half=half // N_KEYS)
    head_shape = (PEER_HEADS, N_KEYS, tm)
    once = pl.Buffered(1)
    return pl.pallas_call(
        kern,
        grid=(b, l // tm, n_chunks + 1),
        in_specs=[pl.BlockSpec((1, tm, d), lambda i, t, j: (i, t, 0)),
                  pl.BlockSpec((1, 8, d), lambda i, t, j: (i, 0, 0)),
                  pl.BlockSpec((1, d), lambda i, t, j: (0, 0)),
                  pl.BlockSpec(wq_t.shape, lambda i, t, j: (0, 0), pipeline_mode=once),
                  pl.BlockSpec(keys.shape, lambda i, t, j: (0, 0, 0, 0), pipeline_mode=once),
                  pl.BlockSpec((chunk, d), lambda i, t, j: (jnp.minimum(j, n_chunks - 1), 0)),
                  pl.BlockSpec((d, chunk), lambda i, t, j: (0, jnp.maximum(j - 1, 0)))],
        out_specs=pl.BlockSpec((1, tm, d), lambda i, t, j: (i, t, 0)),
        out_shape=jax.ShapeDtypeStruct((b, l, d), F32),
        scratch_shapes=[pltpu.VMEM((d, tm), BF16),
                        pltpu.VMEM(head_shape, F32), pltpu.VMEM(head_shape, F32),
                        pltpu.VMEM(head_shape, BF16), pltpu.VMEM(head_shape, BF16),
                        pltpu.VMEM((d, tm), F32),
                        pltpu.VMEM((half, tm), F32), pltpu.VMEM((half, tm), F32),
                        pltpu.VMEM((half, tm), BF16), pltpu.VMEM((half, tm), BF16),
                        pltpu.VMEM((2, N_KEYS, tm), F32),
                        pltpu.VMEM((2, half // N_KEYS * PEER_HEADS, tm), F32),
                        pltpu.VMEM((2, half // N_KEYS * PEER_HEADS, tm), F32)],
        compiler_params=_params(("parallel", "parallel", "arbitrary")),
        name="peer",
    )(x, mods, nw, wq_t, keys, u, v_t)


def _final_norm_kernel(x_ref, w_ref, o_ref):
    x = x_ref[0]
    o_ref[0] = x * lax.rsqrt(jnp.mean(x * x, axis=-1, keepdims=True) + RMS_EPS) * w_ref[...]


def _final_norm(x, w):
    b, l, d = x.shape
    tm = min(512, l)
    return pl.pallas_call(
        _final_norm_kernel,
        grid=(b, l // tm),
        in_specs=[pl.BlockSpec((1, tm, d), lambda i, j: (i, j, 0)),
                  pl.BlockSpec((1, d), lambda i, j: (0, 0))],
        out_specs=pl.BlockSpec((1, tm, d), lambda i, j: (i, j, 0)),
        out_shape=jax.ShapeDtypeStruct((b, l, d), F32),
        compiler_params=_params(("parallel", "parallel")),
        name="final_norm",
    )(x, w)


def _block_diag(w):
    g, n, _ = w.shape
    out = jnp.zeros((g * n, g * n), w.dtype)
    for i in range(g):
        out = out.at[i * n:(i + 1) * n, i * n:(i + 1) * n].set(w[i])
    return out


def _reorder_w_in(w):
    d = w.shape[0]
    return jnp.concatenate([w[:, :REF_QKV], w[:, REF_ALPHA:REF_Z], w[:, REF_Z:REF_POOL],
                            w[:, REF_POOL:REF_FNET], w[:, REF_QKV:REF_ALPHA],
                            jnp.zeros((d, IN_PAD - REF_FNET), w.dtype)], axis=1).astype(BF16)


def _gate_columns(pz, px):
    ba = jnp.concatenate([pz[:, :, COL_BA:COL_BA + 16], px[:, :, COL_BA:COL_BA + 16]], axis=1)
    b, ltot, _ = ba.shape
    return ba.reshape(b, ltot, 4, DN_HEADS).transpose(0, 3, 1, 2)


def _mod_rows(mod, rows):
    d = mod.shape[1] // N_MOD
    m = mod.reshape(16, N_MOD, d)[rows]
    return jnp.concatenate([m, jnp.zeros((m.shape[0], 8 - N_MOD, d), m.dtype)], axis=1)


def kernel(x, c, ctx, c_ctx, w_mod, b_mod, norm1_w, norm2_w, w_in, conv_w, a_log, dt_bias,
           dn_norm_w, w_pool, pool_scale, w_fnet, w_out, w_query, sub_keys, expert_u, expert_v,
           final_norm_w):
    b, l, d = x.shape
    m = ctx.shape[1]
    depth = w_mod.shape[0]
    rows = l // GRID_W

    cs = jnp.concatenate([c, c_ctx[None, :], jnp.zeros((16 - b - 1, d), F32)], axis=0)
    mod_all = _modulation(cs, w_mod, b_mod)

    z = ctx
    for i in range(depth):
        update_ctx = i < depth - 1
        mods_x = _mod_rows(mod_all[i], jnp.arange(b))
        mods_z = _mod_rows(mod_all[i], jnp.full((b,), b))
        w_in_r = _reorder_w_in(w_in[i])
        nw1 = norm1_w[i][None, :]
        nw2 = norm2_w[i][None, :]

        px = _inproj(x, mods_x, nw1, w_in_r)
        pz = _inproj(z, mods_z, nw1, w_in_r)

        cw = conv_w[i].reshape(CONV_WIDTH, 3, DN_HEADS, DN_HEAD_DIM).transpose(2, 1, 0, 3)
        par = jnp.concatenate([a_log[i], dt_bias[i]], axis=0)
        par = jnp.broadcast_to(par.T[:, :, None], (DN_HEADS, 4, LANES))
        par = jnp.concatenate([par, jnp.zeros((DN_HEADS, 4, LANES), F32)], axis=1)
        dn_x, dn_z = _delta(pz, px, _gate_columns(pz, px), cw, par, dn_norm_w[i][None, :])

        wp_bd = _block_diag(w_pool[i]).astype(BF16)
        wf_bd = _block_diag(w_fnet[i])
        w_out_b = w_out[i].astype(BF16)
        ps = pool_scale[i][None, :]

        fn_x = _fnet(px, _fnet_weights(wf_bd, l).astype(BF16))
        x = _mixout(x, dn_x, px, fn_x, l // rows, wp_bd, ps, w_out_b, mods_x)
        if update_ctx:
            fn_z = _fnet(pz, _fnet_weights(wf_bd, m).astype(BF16))
            z = _mixout(z, dn_z, pz, fn_z, m, wp_bd, ps, w_out_b, mods_z)

        wq_t = w_query[i].T.astype(BF16)
        keys = sub_keys[i].astype(BF16)
        u_b = expert_u[i].astype(BF16)
        v_t = expert_v[i].T.astype(BF16)
        x = _peer(x, mods_x, nw2, wq_t, keys, u_b, v_t)
        if update_ctx:
            z = _peer(z, mods_z, nw2, wq_t, keys, u_b, v_t)

    return _final_norm(x, final_norm_w[None, :])
```

```python
import functools
import math

import numpy as np
import jax
import jax.numpy as jnp
from jax import lax
from jax.experimental import pallas as pl
from jax.experimental.pallas import tpu as pltpu

F32 = jnp.float32
BF16 = jnp.bfloat16
HIGHEST = lax.Precision.HIGHEST

D_MODEL = 1024
DN_HEADS = 4
DN_HEAD_DIM = 128
DN_WIDTH = DN_HEADS * DN_HEAD_DIM
POOL_WINDOWS = (2, 4, 8, 16)
POOL_WIDTH = 256
POOL_GROUP_DIM = 64
FNET_WIDTH = 256
FNET_GROUP_DIM = 64
CONV_WIDTH = 4
CHUNK = 64
GRID_W = 64
PEER_HEADS = 8
N_KEYS = 128
N_EXPERTS = N_KEYS * N_KEYS
PEER_TOPK = 16
N_MOD = 6
RMS_EPS = 1e-6
L2_EPS = 1e-6

COL_Q, COL_K, COL_V, COL_Z = 0, 512, 1024, 1536
COL_POOL, COL_FNET, COL_BA = 2048, 2304, 2560
IN_PAD = 2688
REF_QKV, REF_BETA, REF_ALPHA, REF_Z, REF_POOL, REF_FNET = 1536, 1544, 1552, 2064, 2320, 2576

LANES = 128
VMEM_LIMIT = 56 * 1024 * 1024


def _params(sem):
    return pltpu.CompilerParams(dimension_semantics=sem, vmem_limit_bytes=VMEM_LIMIT)


def _mm(a, b):
    return jnp.dot(a.astype(BF16), b.astype(BF16), preferred_element_type=F32)


def _mm_nt(a, b):
    return lax.dot_general(a.astype(BF16), b.astype(BF16), (((1,), (1,)), ((), ())),
                           preferred_element_type=F32)


def _mm_inv(a, b):
    return _mm(a, b)


def _mm_f32(a, b):
    return jnp.dot(a, b, precision=HIGHEST, preferred_element_type=F32)


def _silu(x):
    return x * jax.nn.sigmoid(x)


def _mod_kernel(c_ref, w_ref, b_ref, o_ref):
    o_ref[0] = _mm_f32(_silu(c_ref[...]), w_ref[0]) + b_ref[0]


def _modulation(cs, w_mod, b_mod):
    depth, d, n = w_mod.shape
    tn = 1536
    return pl.pallas_call(
        _mod_kernel,
        grid=(depth, n // tn),
        in_specs=[pl.BlockSpec((16, d), lambda i, j: (0, 0)),
                  pl.BlockSpec((1, d, tn), lambda i, j: (i, 0, j)),
                  pl.BlockSpec((1, 1, tn), lambda i, j: (i, 0, j))],
        out_specs=pl.BlockSpec((1, 16, tn), lambda i, j: (i, 0, j)),
        out_shape=jax.ShapeDtypeStruct((depth, 16, n), F32),
        compiler_params=_params(("parallel", "parallel")),
        name="modulation",
    )(cs, w_mod, b_mod.reshape(depth, 1, n))


def _norm_mod(x, nw, shift, scale):
    y = x * lax.rsqrt(jnp.mean(x * x, axis=-1, keepdims=True) + RMS_EPS) * nw
    return y * (1.0 + scale) + shift


def _inproj_kernel(x_ref, mod_ref, nw_ref, w_ref, o_ref):
    h = _norm_mod(x_ref[0], nw_ref[...], mod_ref[0, 0:1, :], mod_ref[0, 1:2, :])
    o_ref[0] = _mm(h, w_ref[...])


def _inproj(x, mods, nw, w):
    b, l, d = x.shape
    n = w.shape[1]
    tm = min(512, l)
    return pl.pallas_call(
        _inproj_kernel,
        grid=(b, l // tm),
        in_specs=[pl.BlockSpec((1, tm, d), lambda i, j: (i, j, 0)),
                  pl.BlockSpec((1, 8, d), lambda i, j: (i, 0, 0)),
                  pl.BlockSpec((1, d), lambda i, j: (0, 0)),
                  pl.BlockSpec((d, n), lambda i, j: (0, 0))],
        out_specs=pl.BlockSpec((1, tm, n), lambda i, j: (i, j, 0)),
        out_shape=jax.ShapeDtypeStruct((b, l, n), F32),
        compiler_params=_params(("parallel", "parallel")),
        name="inproj",
    )(x, mods, nw, w)


def _delta_kernel(qz_ref, kz_ref, vz_ref, zz_ref, qx_ref, kx_ref, vx_ref, zx_ref,
                  gate_ref, cw_ref, par_ref, dnw_ref, ox_ref, oz_ref,
                  q_s, k_s, v_s, gf_s, gb_s, bf_s, bb_s, o_s, pad_s, kw_s, c_s, qe_s, o0_s, dc_s,
                  *, m_len, l_len):
    ltot = m_len + l_len
    n_ctx = m_len // CHUNK
    n_tot = ltot // CHUNK
    off = CHUNK

    def conv_silu(src_ref, w, length):
        pad_s[off - 8:off, :] = jnp.zeros((8, LANES), F32)
        pad_s[off:off + length, :] = src_ref[0]
        pad_s[off + length:off + length + 8, :] = jnp.zeros((8, LANES), F32)
        y = (w[0:1, :] * pad_s[off - 2:off - 2 + length, :]
             + w[1:2, :] * pad_s[off - 1:off - 1 + length, :]
             + w[2:3, :] * pad_s[off:off + length, :]
             + w[3:4, :] * pad_s[off + 1:off + 1 + length, :])
        return _silu(y)

    def l2norm(t):
        return t * lax.rsqrt(jnp.sum(t * t, axis=-1, keepdims=True) + L2_EPS)

    for (src_q, src_k, src_v, start, length) in ((qz_ref, kz_ref, vz_ref, 0, m_len),
                                                 (qx_ref, kx_ref, vx_ref, m_len, l_len)):
        q_s[start:start + length, :] = l2norm(conv_silu(src_q, cw_ref[0, 0], length)) * (DN_HEAD_DIM ** -0.5)
        k_s[start:start + length, :] = l2norm(conv_silu(src_k, cw_ref[0, 1], length))
        v_s[start:start + length, :] = conv_silu(src_v, cw_ref[0, 2], length)

    gt = gate_ref[0, 0]
    par = par_ref[0]
    shape = (ltot, LANES)
    bf_s[...] = jnp.broadcast_to(jax.nn.sigmoid(gt[:, 0:1]), shape)
    bb_s[...] = jnp.broadcast_to(jax.nn.sigmoid(gt[:, 1:2]), shape)

    def softplus(t):
        return jnp.maximum(t, 0.0) + jnp.log1p(jnp.exp(-jnp.abs(t)))

    gf_s[...] = jnp.broadcast_to(-jnp.exp(par[0:1, 0:1]) * softplus(gt[:, 2:3] + par[2:3, 0:1]), shape)
    gb_s[...] = jnp.broadcast_to(-jnp.exp(par[1:2, 0:1]) * softplus(gt[:, 3:4] + par[3:4, 0:1]), shape)

    pos = lax.broadcasted_iota(jnp.int32, shape, 0) & (CHUNK - 1)
    pad_s[0:off, :] = jnp.zeros((off, LANES), F32)
    pad_s[off + ltot:off + ltot + off, :] = jnp.zeros((off, LANES), F32)
    step = 1
    while step < CHUNK:
        pad_s[off:off + ltot, :] = gf_s[...]
        gf_s[...] = gf_s[...] + jnp.where(pos >= step, pad_s[off - step:off - step + ltot, :], 0.0)
        pad_s[off:off + ltot, :] = gb_s[...]
        gb_s[...] = gb_s[...] + jnp.where(pos < CHUNK - step, pad_s[off + step:off + step + ltot, :], 0.0)
        step *= 2

    ri = lax.broadcasted_iota(jnp.int32, (CHUNK, CHUNK), 0)
    ci = lax.broadcasted_iota(jnp.int32, (CHUNK, CHUNK), 1)
    eye = (ri == ci).astype(F32)
    hd = DN_HEAD_DIM
    pair_mask = {1 << b: jnp.logical_and((ri >> (b + 1)) == (ci >> (b + 1)), (ri >> b) != (ci >> b))
                 for b in range(6)}

    group = 4 if n_tot % 4 == 0 else 2

    def stage1(g, carry):
        chains = [(g * group + c, d) for c in range(group) for d in (0, 1)]
        both = lambda fn: [fn(n, d) for (n, d) in chains]
        each = lambda fn, *cols: [fn(*vals) for vals in zip(*cols)]
        rows = lambda n: pl.ds(pl.multiple_of(n * CHUNK, CHUNK), CHUNK)
        incl = both(lambda n, d: (ri >= ci) if d == 0 else (ri <= ci))
        strict = both(lambda n, d: (ri > ci) if d == 0 else (ri < ci))
        q = both(lambda n, d: q_s[rows(n), :])
        k = both(lambda n, d: k_s[rows(n), :])
        v = both(lambda n, d: v_s[rows(n), :])
        gi = both(lambda n, d: (gf_s if d == 0 else gb_s)[rows(n), :])
        bi = both(lambda n, d: (bf_s if d == 0 else bb_s)[rows(n), :])
        diff = each(lambda g_: g_[:, 0:CHUNK] - g_.T[0:CHUNK, :], gi)
        decay = each(lambda m, df: jnp.where(m, jnp.exp(jnp.where(m, df, 0.0)), 0.0), incl, diff)
        kb = each(lambda k_, b_: k_ * b_, k, bi)
        kk = each(_mm_nt, kb, k)
        low = each(lambda m, kk_, dec: jnp.where(m, kk_ * dec, 0.0), strict, kk, decay)
        inv = [eye - jnp.where(pair_mask[1], l_, 0.0) for l_ in low]
        size = 2
        while size < CHUNK:
            coupling = [jnp.where(pair_mask[size], l_, 0.0) for l_ in low]
            left = each(lambda a_, b_: _mm_inv(a_, b_), inv, coupling)
            corr = each(lambda a_, b_: _mm_inv(a_, b_), left, inv)
            inv = each(lambda t, u_: t - u_, inv, corr)
            size *= 2
        eg = [jnp.exp(g_) for g_ in gi]
        rhs = each(lambda kb_, e_, v_, b_: jnp.concatenate([kb_ * e_, v_ * b_], axis=1), kb, eg, v, bi)
        sol = each(_mm, inv, rhs)
        qk = each(_mm_nt, q, k)
        intra = each(lambda m, qk_, dec: jnp.where(m, qk_ * dec, 0.0), incl, qk, decay)
        g_last = [g_[CHUNK - 1:CHUNK, :] if d == 0 else g_[0:1, :] for g_, (_, d) in zip(gi, chains)]
        k_tail_t = each(lambda k_, gl, g_: (k_ * jnp.exp(gl - g_)).T, k, g_last, gi)
        iwu = each(_mm, intra, sol)
        kwu = each(_mm, k_tail_t, sol)
        for idx, (n, d) in enumerate(chains):
            sk = pl.ds(pl.multiple_of(n * hd, hd), hd)
            qe_s[d, rows(n), :] = (q[idx] * eg[idx] - iwu[idx][:, 0:hd]).astype(BF16)
            o0_s[d, rows(n), :] = iwu[idx][:, hd:]
            kw_s[d, sk, :] = kwu[idx][:, 0:hd].astype(BF16)
            c_s[d, sk, :] = kwu[idx][:, hd:]
            dc_s[d, pl.ds(pl.multiple_of(n * 8, 8), 8), :] = jnp.broadcast_to(jnp.exp(g_last[idx]), (8, LANES))
        return carry

    lax.fori_loop(0, n_tot // group, stage1, 0)

    def advance(direction, n, state):
        sl = pl.ds(pl.multiple_of(n * CHUNK, CHUNK), CHUNK)
        sk = pl.ds(pl.multiple_of(n * hd, hd), hd)
        sb = state.astype(BF16)
        o_s[direction, sl, :] = (jnp.dot(qe_s[direction, sl, :], sb, preferred_element_type=F32)
                                 + o0_s[direction, sl, :])
        dc = dc_s[direction, pl.ds(pl.multiple_of(n * 8, 8), 1), :]
        return (state * dc - jnp.dot(kw_s[direction, sk, :], sb, preferred_element_type=F32)
                + c_s[direction, sk, :])

    def stage2(t, states):
        n_b = jnp.where(t < n_ctx, n_ctx - 1 - t, n_tot + n_ctx - 1 - t)
        return advance(0, t, states[0]), advance(1, n_b, states[1])

    zero = jnp.zeros((hd, hd), F32)
    lax.fori_loop(0, n_tot, stage2, (zero, zero))

    o = o_s[0] + o_s[1]
    o = o * lax.rsqrt(jnp.mean(o * o, axis=-1, keepdims=True) + RMS_EPS) * dnw_ref[...]
    oz_ref[0] = o[0:m_len, :] * _silu(zz_ref[0])
    ox_ref[0] = o[m_len:, :] * _silu(zx_ref[0])


def _delta(pz, px, gates, conv_w, par, dn_norm_w):
    b, m_len, _ = pz.shape
    l_len = px.shape[1]
    ltot = m_len + l_len
    n_tot = ltot // CHUNK
    hb = LANES

    def col(c0, length):
        return pl.BlockSpec((1, length, hb), lambda i, h: (i, 0, c0 // hb + h))

    scratch = [pltpu.VMEM((ltot, LANES), F32) for _ in range(7)]
    scratch += [pltpu.VMEM((2, ltot, LANES), F32),
                pltpu.VMEM((ltot + 2 * CHUNK, LANES), F32),
                pltpu.VMEM((2, n_tot * DN_HEAD_DIM, LANES), BF16),
                pltpu.VMEM((2, n_tot * DN_HEAD_DIM, LANES), F32),
                pltpu.VMEM((2, ltot, LANES), BF16),
                pltpu.VMEM((2, ltot, LANES), F32),
                pltpu.VMEM((2, n_tot * 8, LANES), F32)]
    return pl.pallas_call(
        functools.partial(_delta_kernel, m_len=m_len, l_len=l_len),
        grid=(b, DN_HEADS),
        in_specs=[col(COL_Q, m_len), col(COL_K, m_len), col(COL_V, m_len), col(COL_Z, m_len),
                  col(COL_Q, l_len), col(COL_K, l_len), col(COL_V, l_len), col(COL_Z, l_len),
                  pl.BlockSpec((1, 1, ltot, 4), lambda i, h: (i, h, 0, 0)),
                  pl.BlockSpec((1, 3, CONV_WIDTH, hb), lambda i, h: (h, 0, 0, 0)),
                  pl.BlockSpec((1, 8, hb), lambda i, h: (h, 0, 0)),
                  pl.BlockSpec((1, hb), lambda i, h: (0, 0))],
        out_specs=[pl.BlockSpec((1, l_len, hb), lambda i, h: (i, 0, h)),
                   pl.BlockSpec((1, m_len, hb), lambda i, h: (i, 0, h))],
        out_shape=[jax.ShapeDtypeStruct((b, l_len, DN_WIDTH), F32),
                   jax.ShapeDtypeStruct((b, m_len, DN_WIDTH), F32)],
        scratch_shapes=scratch,
        compiler_params=_params(("parallel", "parallel")),
        name="delta",
    )(pz, pz, pz, pz, px, px, px, px, gates, conv_w, par, dn_norm_w)


def _fnet_weight_kernel(c_ref, s_ref, w_ref, o_ref, *, scale):
    w = w_ref[...]
    o_ref[:, 0:FNET_WIDTH] = _mm_f32(c_ref[...], w) * scale
    o_ref[:, FNET_WIDTH:] = _mm_f32(s_ref[...], w) * scale


def _fnet_weights(w_bd, seq_len):
    n = FNET_GROUP_DIM
    ang = 2.0 * np.pi * ((np.arange(n)[:, None] * np.arange(n)[None, :]) % n) / n
    eye4 = np.eye(FNET_WIDTH // n)
    c_bd = jnp.asarray(np.kron(eye4, np.cos(ang)), F32)
    s_bd = jnp.asarray(np.kron(eye4, np.sin(ang)), F32)
    return pl.pallas_call(
        functools.partial(_fnet_weight_kernel, scale=float(1.0 / math.sqrt(seq_len * n))),
        out_shape=jax.ShapeDtypeStruct((FNET_WIDTH, 2 * FNET_WIDTH), F32),
        name="fnet_weights",
    )(c_bd, s_bd, w_bd)


def _fnet_kernel(x_ref, w_ref, c_ref, s_ref, o_ref):
    xw = _mm(x_ref[0], w_ref[...])
    o_ref[0] = _mm(c_ref[...], xw[:, 0:FNET_WIDTH]) + _mm(s_ref[...], xw[:, FNET_WIDTH:])


def _dft_mats(seq_len):
    idx = (np.arange(seq_len)[:, None] * np.arange(seq_len)[None, :]) % seq_len
    ang = 2.0 * np.pi * idx / seq_len
    return jnp.asarray(np.cos(ang), F32).astype(BF16), jnp.asarray(-np.sin(ang), F32).astype(BF16)


def _fnet(p, wcs):
    b, l, _ = p.shape
    tm = min(512, l)
    cm, sm = _dft_mats(l)
    return pl.pallas_call(
        _fnet_kernel,
        grid=(l // tm, b),
        in_specs=[pl.BlockSpec((1, l, FNET_WIDTH), lambda i, j: (j, 0, COL_FNET // FNET_WIDTH)),
                  pl.BlockSpec((FNET_WIDTH, 2 * FNET_WIDTH), lambda i, j: (0, 0)),
                  pl.BlockSpec((tm, l), lambda i, j: (i, 0)),
                  pl.BlockSpec((tm, l), lambda i, j: (i, 0))],
        out_specs=pl.BlockSpec((1, tm, FNET_WIDTH), lambda i, j: (j, i, 0)),
        out_shape=jax.ShapeDtypeStruct((b, l, FNET_WIDTH), F32),
        compiler_params=_params(("parallel", "parallel")),
        name="fnet",
    )(p, wcs, cm, sm)


def _mixout_kernel(x_ref, dn_ref, p_ref, f_ref, band_ref, icnt_ref, wp_ref, ps_ref, wo_ref,
                   mod_ref, o_ref):
    xp = p_ref[0]
    group = lax.broadcasted_iota(jnp.int32, xp.shape, 1) // POOL_GROUP_DIM
    win_sum = jnp.zeros(xp.shape, F32)
    for g in range(len(POOL_WINDOWS)):
        win_sum = win_sum + _mm(band_ref[g], jnp.where(group == g, xp, 0.0))
    y = win_sum * icnt_ref[...] - xp
    pool = _mm(y, wp_ref[...]) * ps_ref[...]
    out = (_mm(dn_ref[0], wo_ref[0:DN_WIDTH, :])
           + _mm(pool, wo_ref[DN_WIDTH:DN_WIDTH + POOL_WIDTH, :])
           + _mm(f_ref[0], wo_ref[DN_WIDTH + POOL_WIDTH:, :]))
    o_ref[0] = x_ref[0] + mod_ref[0, 2:3, :] * out


def _pool_consts(tm, seg):
    pos = np.arange(tm) % seg
    seg_id = np.arange(tm) // seg
    band = np.zeros((len(POOL_WINDOWS), tm, tm), np.float32)
    icnt = np.zeros((tm, POOL_WIDTH), np.float32)
    for g, w in enumerate(POOL_WINDOWS):
        lo = np.clip(pos - w // 2, 0, seg)
        hi = np.clip(pos + w - w // 2, 0, seg)
        j = np.arange(tm)
        inside = (seg_id[:, None] == seg_id[None, :]) & (pos[None, :] >= lo[:, None]) & (pos[None, :] < hi[:, None])
        band[g] = inside.astype(np.float32)
        icnt[:, g * POOL_GROUP_DIM:(g + 1) * POOL_GROUP_DIM] = (1.0 / (hi - lo))[:, None]
    return jnp.asarray(band).astype(BF16), jnp.asarray(icnt)


def _mixout(x, dn, p, fn, seg, wp_bd, pool_scale, w_out, mods):
    b, l, d = x.shape
    tm = min(256, l)
    band, icnt = _pool_consts(tm, min(seg, tm))
    if seg > tm:
        raise ValueError("pooling segment longer than the token tile")
    return pl.pallas_call(
        _mixout_kernel,
        grid=(b, l // tm),
        in_specs=[pl.BlockSpec((1, tm, d), lambda i, j: (i, j, 0)),
                  pl.BlockSpec((1, tm, DN_WIDTH), lambda i, j: (i, j, 0)),
                  pl.BlockSpec((1, tm, POOL_WIDTH), lambda i, j: (i, j, COL_POOL // POOL_WIDTH)),
                  pl.BlockSpec((1, tm, FNET_WIDTH), lambda i, j: (i, j, 0)),
                  pl.BlockSpec((len(POOL_WINDOWS), tm, tm), lambda i, j: (0, 0, 0)),
                  pl.BlockSpec((tm, POOL_WIDTH), lambda i, j: (0, 0)),
                  pl.BlockSpec((POOL_WIDTH, POOL_WIDTH), lambda i, j: (0, 0)),
                  pl.BlockSpec((1, POOL_WIDTH), lambda i, j: (0, 0)),
                  pl.BlockSpec((d, d), lambda i, j: (0, 0)),
                  pl.BlockSpec((1, 8, d), lambda i, j: (i, 0, 0))],
        out_specs=pl.BlockSpec((1, tm, d), lambda i, j: (i, j, 0)),
        out_shape=jax.ShapeDtypeStruct((b, l, d), F32),
        compiler_params=_params(("parallel", "parallel")),
        name="mixout",
    )(x, dn, p, fn, band, icnt, wp_bd, pool_scale, w_out, mods)


def _peer_kernel(x_ref, mod_ref, nw_ref, wq_ref, keys_ref, u_ref, vt_ref, o_ref,
                 ht_s, c1_s, e1_s, r2_s, e2_s, acc_s, act_s, acta_s, wga_s, wgb_s, sc_s, rc_s, re_s,
                 *, n_chunks, blocks_per_half):
    p = pl.program_id(2)
    neg_inf = float("-inf")
    half = blocks_per_half * N_KEYS
    n_lane_groups = x_ref.shape[1] // LANES
    d_model = x_ref.shape[2]

    @pl.when(p == 0)
    def _route():
        h = _norm_mod(x_ref[0], nw_ref[...], mod_ref[0, 3:4, :], mod_ref[0, 4:5, :])
        ht_s[...] = h.T.astype(BF16)

        marker = 2.0 ** 100

        def extract(s):
            vals = []
            for r in range(PEER_TOPK):
                mx = jnp.max(s, axis=0, keepdims=True)
                vals.append(mx)
                s = jnp.where(s == mx, -marker * (r + 1), s)
            rank = jnp.where(s < -0.5 * marker, s * (-1.0 / marker) - 1.0, 99.0)
            return vals, rank

        pairs = [(a, b) for a in range(PEER_TOPK) for b in range(PEER_TOPK)
                 if (a + 1) * (b + 1) <= PEER_TOPK]
        n_pad = (-len(pairs)) % 8

        for hd in range(PEER_HEADS):
            qt = jnp.dot(wq_ref[hd * 2 * N_KEYS:(hd + 1) * 2 * N_KEYS, :], ht_s[...],
                         preferred_element_type=F32)
            sc_s[0] = _mm(keys_ref[hd, 0], qt[0:N_KEYS, :])
            sc_s[1] = _mm(keys_ref[hd, 1], qt[N_KEYS:, :])
            for tg in range(n_lane_groups):
                ln = slice(tg * LANES, (tg + 1) * LANES)
                s1 = sc_s[0, :, ln]
                s2 = sc_s[1, :, ln]
                t1, rank1 = extract(s1)
                t2, rank2 = extract(s2)
                cands = [t1[a] + t2[b] for a, b in pairs]
                cand = jnp.concatenate(cands + [jnp.full_like(cands[0], neg_inf)] * n_pad, axis=0)
                work = cand
                thr = jnp.full_like(cands[0], neg_inf)
                found = jnp.zeros_like(cands[0])
                for _ in range(PEER_TOPK):
                    mx = jnp.max(work, axis=0, keepdims=True)
                    cnt = jnp.sum(jnp.where(cand >= mx, 1.0, 0.0), axis=0, keepdims=True)
                    hit = jnp.where(cnt >= PEER_TOPK, 1.0 - found, 0.0)
                    thr = jnp.where(hit > 0.0, mx, thr)
                    found = jnp.maximum(found, hit)
                    work = jnp.where(work == mx, neg_inf, work)
                z = jnp.sum(jnp.where(cand >= thr, jnp.exp(cand - cands[0]), 0.0), axis=0, keepdims=True)
                cnt1 = jnp.zeros(s1.shape, F32)
                for a in range(PEER_TOPK):
                    partners = jnp.zeros_like(thr)
                    for idx, (pa, _) in enumerate(pairs):
                        if pa == a:
                            partners = partners + jnp.where(cands[idx] >= thr, 1.0, 0.0)
                    cnt1 = jnp.where(rank1 == float(a), partners, cnt1)
                c1_s[hd, :, ln] = cnt1
                e1_s[hd, :, ln] = jnp.exp(s1 - t1[0]) * (1.0 / z)
                r2_s[hd, :, ln] = rank2.astype(BF16)
                e2_s[hd, :, ln] = jnp.exp(s2 - t2[0]).astype(BF16)
        acc_s[...] = jnp.zeros(acc_s.shape, F32)
        act_s[...] = jnp.zeros(act_s.shape, F32)
        wga_s[...] = jnp.zeros(wga_s.shape, BF16)

    def row_bf16(row):
        return pltpu.repeat(jnp.broadcast_to(row, (16, LANES)).astype(BF16), N_KEYS // 16, axis=0)

    def stage_rows(first_block, slot):
        for ii in range(blocks_per_half):
            i = jnp.maximum(first_block + ii, 0)
            for hd in range(PEER_HEADS):
                k = ii * PEER_HEADS + hd
                rc_s[slot, k:k + 1, :] = c1_s[hd, pl.ds(i, 1), :]
                re_s[slot, k:k + 1, :] = e1_s[hd, pl.ds(i, 1), :]

    def gate_unit(act, slot, out_ref, tg, g0):
        ln = slice(tg * LANES, (tg + 1) * LANES)
        blocks = list(range(g0, min(g0 + 4, blocks_per_half)))
        gates = [jnp.zeros((N_KEYS, LANES), BF16) for _ in blocks]
        for hd in range(PEER_HEADS):
            r2 = r2_s[hd, :, ln]
            e2 = e2_s[hd, :, ln]
            for n, ii in enumerate(blocks):
                k = ii * PEER_HEADS + hd
                c1row = row_bf16(rc_s[slot, k:k + 1, ln])
                e1row = row_bf16(re_s[slot, k:k + 1, ln])
                gates[n] = gates[n] + jnp.minimum(jnp.maximum(c1row - r2, 0.0), e1row) * e2
        for n, ii in enumerate(blocks):
            a = act[ii * N_KEYS:(ii + 1) * N_KEYS, ln]
            gelu = 0.5 * a * (1.0 + lax.erf(a * (2.0 ** -0.5)))
            out_ref[ii * N_KEYS:(ii + 1) * N_KEYS, ln] = gates[n] * gelu.astype(BF16)

    def phase(matmul_piece, act, slot, out_ref, lane_groups):
        units = [(tg, g0) for tg in lane_groups for g0 in range(0, blocks_per_half, 4)]
        for idx, unit in enumerate(units):
            matmul_piece(idx, len(units))
            gate_unit(act, slot, out_ref, *unit)

    def value_piece(lo, w_ref):
        def piece(idx, n_pieces):
            rows = d_model // n_pieces
            r0 = idx * rows
            acc_s[r0:r0 + rows, :] += jnp.dot(vt_ref[r0:r0 + rows, lo:lo + half], w_ref[...],
                                              preferred_element_type=F32)
        return piece

    def act_piece(lo, dst_ref):
        def piece(idx, n_pieces):
            rows = half // n_pieces
            r0 = idx * rows
            dst_ref[r0:r0 + rows, :] = jnp.dot(u_ref[lo + r0:lo + r0 + rows, :], ht_s[...],
                                               preferred_element_type=F32)
        return piece

    prev_b = (2 * p - 1) * blocks_per_half

    first = tuple(range(n_lane_groups // 2))
    second = tuple(range(n_lane_groups // 2, n_lane_groups))

    @pl.when(p < n_chunks)
    def _stage():
        stage_rows(prev_b, 0)
        stage_rows(2 * p * blocks_per_half, 1)

    @pl.when(p + 1 <= n_chunks)
    def _phase1():
        phase(value_piece(0, wga_s), act_s, 0, wgb_s, first)

    @pl.when(p + 2 <= n_chunks + 1)
    def _phase2():
        phase(act_piece(0, acta_s), act_s, 0, wgb_s, second)

    @pl.when(p + 3 <= n_chunks + 2)
    def _phase3():
        phase(value_piece(half, wgb_s), acta_s, 1, wga_s, first)

    @pl.when(p + 4 <= n_chunks + 3)
    def _phase4():
        phase(act_piece(half, act_s), acta_s, 1, wga_s, second)

    @pl.when(p == n_chunks)
    def _drain1():
        stage_rows(prev_b, 0)
        phase(value_piece(0, wga_s), act_s, 0, wgb_s, first + second)

    @pl.when(p >= n_chunks)
    def _drain2():
        out = acc_s[...] + jnp.dot(vt_ref[:, half:], wgb_s[...], preferred_element_type=F32)
        o_ref[0] = x_ref[0] + mod_ref[0, 5:6, :] * out.T


def _peer(x, mods, nw, wq_t, keys, u, v_t, tm=512, chunk=2048):
    b, l, d = x.shape
    tm = min(tm, l)
    n_chunks = N_EXPERTS // chunk
    half = chunk // 2
    kern = functools.partial(_peer_kernel, n_chunks=n_chunks, blocks_per_half=half // N_KEYS)
    head_shape = (PEER_HEADS, N_KEYS, tm)
    once = pl.Buffered(1)
    return pl.pallas_call(
        kern,
        grid=(b, l // tm, n_chunks + 1),
        in_specs=[pl.BlockSpec((1, tm, d), lambda i, t, j: (i, t, 0)),
                  pl.BlockSpec((1, 8, d), lambda i, t, j: (i, 0, 0)),
                  pl.BlockSpec((1, d), lambda i, t, j: (0, 0)),
                  pl.BlockSpec(wq_t.shape, lambda i, t, j: (0, 0), pipeline_mode=once),
                  pl.BlockSpec(keys.shape, lambda i, t, j: (0, 0, 0, 0), pipeline_mode=once),
                  pl.BlockSpec((chunk, d), lambda i, t, j: (jnp.minimum(j, n_chunks - 1), 0)),
                  pl.BlockSpec((d, chunk), lambda i, t, j: (0, jnp.maximum(j - 1, 0)))],
        out_specs=pl.BlockSpec((1, tm, d), lambda i, t, j: (i, t, 0)),
        out_shape=jax.ShapeDtypeStruct((b, l, d), F32),
        scratch_shapes=[pltpu.VMEM((d, tm), BF16),
                        pltpu.VMEM(head_shape, F32), pltpu.VMEM(head_shape, F32),
                        pltpu.VMEM(head_shape, BF16), pltpu.VMEM(head_shape, BF16),
                        pltpu.VMEM((d, tm), F32),
                        pltpu.VMEM((half, tm), F32), pltpu.VMEM((half, tm), F32),
                        pltpu.VMEM((half, tm), BF16), pltpu.VMEM((half, tm), BF16),
                        pltpu.VMEM((2, N_KEYS, tm), F32),
                        pltpu.VMEM((2, half // N_KEYS * PEER_HEADS, tm), F32),
                        pltpu.VMEM((2, half // N_KEYS * PEER_HEADS, tm), F32)],
        compiler_params=_params(("parallel", "parallel", "arbitrary")),
        name="peer",
    )(x, mods, nw, wq_t, keys, u, v_t)


def _final_norm_kernel(x_ref, w_ref, o_ref):
    x = x_ref[0]
    o_ref[0] = x * lax.rsqrt(jnp.mean(x * x, axis=-1, keepdims=True) + RMS_EPS) * w_ref[...]


def _final_norm(x, w):
    b, l, d = x.shape
    tm = min(512, l)
    return pl.pallas_call(
        _final_norm_kernel,
        grid=(b, l // tm),
        in_specs=[pl.BlockSpec((1, tm, d), lambda i, j: (i, j, 0)),
                  pl.BlockSpec((1, d), lambda i, j: (0, 0))],
        out_specs=pl.BlockSpec((1, tm, d), lambda i, j: (i, j, 0)),
        out_shape=jax.ShapeDtypeStruct((b, l, d), F32),
        compiler_params=_params(("parallel", "parallel")),
        name="final_norm",
    )(x, w)


def _block_diag(w):
    g, n, _ = w.shape
    out = jnp.zeros((g * n, g * n), w.dtype)
    for i in range(g):
        out = out.at[i * n:(i + 1) * n, i * n:(i + 1) * n].set(w[i])
    return out


def _reorder_w_in(w):
    d = w.shape[0]
    return jnp.concatenate([w[:, :REF_QKV], w[:, REF_ALPHA:REF_Z], w[:, REF_Z:REF_POOL],
                            w[:, REF_POOL:REF_FNET], w[:, REF_QKV:REF_ALPHA],
                            jnp.zeros((d, IN_PAD - REF_FNET), w.dtype)], axis=1).astype(BF16)


def _gate_columns(pz, px):
    ba = jnp.concatenate([pz[:, :, COL_BA:COL_BA + 16], px[:, :, COL_BA:COL_BA + 16]], axis=1)
    b, ltot, _ = ba.shape
    return ba.reshape(b, ltot, 4, DN_HEADS).transpose(0, 3, 1, 2)


def _mod_rows(mod, rows):
    d = mod.shape[1] // N_MOD
    m = mod.reshape(16, N_MOD, d)[rows]
    return jnp.concatenate([m, jnp.zeros((m.shape[0], 8 - N_MOD, d), m.dtype)], axis=1)


def kernel(x, c, ctx, c_ctx, w_mod, b_mod, norm1_w, norm2_w, w_in, conv_w, a_log, dt_bias,
           dn_norm_w, w_pool, pool_scale, w_fnet, w_out, w_query, sub_keys, expert_u, expert_v,
           final_norm_w):
    b, l, d = x.shape
    m = ctx.shape[1]
    depth = w_mod.shape[0]
    rows = l // GRID_W

    cs = jnp.concatenate([c, c_ctx[None, :], jnp.zeros((16 - b - 1, d), F32)], axis=0)
    mod_all = _modulation(cs, w_mod, b_mod)

    z = ctx
    for i in range(depth):
        update_ctx = i < depth - 1
        mods_x = _mod_rows(mod_all[i], jnp.arange(b))
        mods_z = _mod_rows(mod_all[i], jnp.full((b,), b))
        w_in_r = _reorder_w_in(w_in[i])
        nw1 = norm1_w[i][None, :]
        nw2 = norm2_w[i][None, :]

        px = _inproj(x, mods_x, nw1, w_in_r)
        pz = _inproj(z, mods_z, nw1, w_in_r)

        cw = conv_w[i].reshape(CONV_WIDTH, 3, DN_HEADS, DN_HEAD_DIM).transpose(2, 1, 0, 3)
        par = jnp.concatenate([a_log[i], dt_bias[i]], axis=0)
        par = jnp.broadcast_to(par.T[:, :, None], (DN_HEADS, 4, LANES))
        par = jnp.concatenate([par, jnp.zeros((DN_HEADS, 4, LANES), F32)], axis=1)
        dn_x, dn_z = _delta(pz, px, _gate_columns(pz, px), cw, par, dn_norm_w[i][None, :])

        wp_bd = _block_diag(w_pool[i]).astype(BF16)
        wf_bd = _block_diag(w_fnet[i])
        w_out_b = w_out[i].astype(BF16)
        ps = pool_scale[i][None, :]

        fn_x = _fnet(px, _fnet_weights(wf_bd, l).astype(BF16))
        x = _mixout(x, dn_x, px, fn_x, l // rows, wp_bd, ps, w_out_b, mods_x)
        if update_ctx:
            fn_z = _fnet(pz, _fnet_weights(wf_bd, m).astype(BF16))
            z = _mixout(z, dn_z, pz, fn_z, m, wp_bd, ps, w_out_b, mods_z)

        wq_t = w_query[i].T.astype(BF16)
        keys = sub_keys[i].astype(BF16)
        u_b = expert_u[i].astype(BF16)
        v_t = expert_v[i].T.astype(BF16)
        x = _peer(x, mods_x, nw2, wq_t, keys, u_b, v_t)
        if update_ctx:
            z = _peer(z, mods_z, nw2, wq_t, keys, u_b, v_t)

    return _final_norm(x, final_norm_w[None, :])
```

```python
import functools
import math

import numpy as np
import jax
import jax.numpy as jnp
from jax import lax
from jax.experimental import pallas as pl
from jax.experimental.pallas import tpu as pltpu

F32 = jnp.float32
BF16 = jnp.bfloat16
HIGHEST = lax.Precision.HIGHEST

D_MODEL = 1024
DN_HEADS = 4
DN_HEAD_DIM = 128
DN_WIDTH = DN_HEADS * DN_HEAD_DIM
POOL_WINDOWS = (2, 4, 8, 16)
POOL_WIDTH = 256
POOL_GROUP_DIM = 64
FNET_WIDTH = 256
FNET_GROUP_DIM = 64
CONV_WIDTH = 4
CHUNK = 64
GRID_W = 64
PEER_HEADS = 8
N_KEYS = 128
N_EXPERTS = N_KEYS * N_KEYS
PEER_TOPK = 16
N_MOD = 6
RMS_EPS = 1e-6
L2_EPS = 1e-6

COL_Q, COL_K, COL_V, COL_Z = 0, 512, 1024, 1536
COL_POOL, COL_FNET, COL_BA = 2048, 2304, 2560
IN_PAD = 2688
REF_QKV, REF_BETA, REF_ALPHA, REF_Z, REF_POOL, REF_FNET = 1536, 1544, 1552, 2064, 2320, 2576

LANES = 128
VMEM_LIMIT = 56 * 1024 * 1024


def _params(sem, flags=None):
    return pltpu.CompilerParams(dimension_semantics=sem, vmem_limit_bytes=VMEM_LIMIT, flags=flags)


def _mm(a, b):
    return jnp.dot(a.astype(BF16), b.astype(BF16), preferred_element_type=F32)


def _mm_nt(a, b):
    return lax.dot_general(a.astype(BF16), b.astype(BF16), (((1,), (1,)), ((), ())),
                           preferred_element_type=F32)


def _mm_inv(a, b):
    return _mm(a, b)


def _mm_f32(a, b):
    return jnp.dot(a, b, precision=HIGHEST, preferred_element_type=F32)


def _silu(x):
    return x * jax.nn.sigmoid(x)


def _mod_kernel(c_ref, w_ref, b_ref, o_ref):
    o_ref[0] = _mm_f32(_silu(c_ref[...]), w_ref[0]) + b_ref[0]


def _modulation(cs, w_mod, b_mod):
    depth, d, n = w_mod.shape
    tn = 1536
    return pl.pallas_call(
        _mod_kernel,
        grid=(depth, n // tn),
        in_specs=[pl.BlockSpec((16, d), lambda i, j: (0, 0)),
                  pl.BlockSpec((1, d, tn), lambda i, j: (i, 0, j)),
                  pl.BlockSpec((1, 1, tn), lambda i, j: (i, 0, j))],
        out_specs=pl.BlockSpec((1, 16, tn), lambda i, j: (i, 0, j)),
        out_shape=jax.ShapeDtypeStruct((depth, 16, n), F32),
        compiler_params=_params(("parallel", "parallel")),
        name="modulation",
    )(cs, w_mod, b_mod.reshape(depth, 1, n))


def _norm_mod(x, nw, shift, scale):
    y = x * lax.rsqrt(jnp.mean(x * x, axis=-1, keepdims=True) + RMS_EPS) * nw
    return y * (1.0 + scale) + shift


def _inproj_kernel(x_ref, mod_ref, nw_ref, w_ref, o_ref):
    h = _norm_mod(x_ref[0], nw_ref[...], mod_ref[0, 0:1, :], mod_ref[0, 1:2, :])
    o_ref[0] = _mm(h, w_ref[...])


def _inproj(x, mods, nw, w):
    b, l, d = x.shape
    n = w.shape[1]
    tm = min(512, l)
    return pl.pallas_call(
        _inproj_kernel,
        grid=(b, l // tm),
        in_specs=[pl.BlockSpec((1, tm, d), lambda i, j: (i, j, 0)),
                  pl.BlockSpec((1, 8, d), lambda i, j: (i, 0, 0)),
                  pl.BlockSpec((1, d), lambda i, j: (0, 0)),
                  pl.BlockSpec((d, n), lambda i, j: (0, 0))],
        out_specs=pl.BlockSpec((1, tm, n), lambda i, j: (i, j, 0)),
        out_shape=jax.ShapeDtypeStruct((b, l, n), F32),
        compiler_params=_params(("parallel", "parallel")),
        name="inproj",
    )(x, mods, nw, w)


def _delta_kernel(qz_ref, kz_ref, vz_ref, zz_ref, qx_ref, kx_ref, vx_ref, zx_ref,
                  gate_ref, cw_ref, par_ref, dnw_ref, ox_ref, oz_ref,
                  q_s, k_s, v_s, gf_s, gb_s, bf_s, bb_s, o_s, pad_s, kw_s, c_s, qe_s, o0_s, dc_s,
                  *, m_len, l_len):
    ltot = m_len + l_len
    n_ctx = m_len // CHUNK
    n_tot = ltot // CHUNK
    off = CHUNK

    def conv_silu(src_ref, w, length):
        pad_s[off - 8:off, :] = jnp.zeros((8, LANES), F32)
        pad_s[off:off + length, :] = src_ref[0]
        pad_s[off + length:off + length + 8, :] = jnp.zeros((8, LANES), F32)
        y = (w[0:1, :] * pad_s[off - 2:off - 2 + length, :]
             + w[1:2, :] * pad_s[off - 1:off - 1 + length, :]
             + w[2:3, :] * pad_s[off:off + length, :]
             + w[3:4, :] * pad_s[off + 1:off + 1 + length, :])
        return _silu(y)

    def l2norm(t):
        return t * lax.rsqrt(jnp.sum(t * t, axis=-1, keepdims=True) + L2_EPS)

    for (src_q, src_k, src_v, start, length) in ((qz_ref, kz_ref, vz_ref, 0, m_len),
                                                 (qx_ref, kx_ref, vx_ref, m_len, l_len)):
        q_s[start:start + length, :] = l2norm(conv_silu(src_q, cw_ref[0, 0], length)) * (DN_HEAD_DIM ** -0.5)
        k_s[start:start + length, :] = l2norm(conv_silu(src_k, cw_ref[0, 1], length))
        v_s[start:start + length, :] = conv_silu(src_v, cw_ref[0, 2], length)

    gt = gate_ref[0, 0]
    par = par_ref[0]
    shape = (ltot, LANES)
    def softplus(t):
        return jnp.maximum(t, 0.0) + jnp.log1p(jnp.exp(-jnp.abs(t)))

    beta = jax.nn.sigmoid(gt)
    a_log = jnp.concatenate([par[0:1, 0:2], par[0:1, 0:1], par[1:2, 0:1]], axis=1)
    dt_bias = jnp.concatenate([par[0:1, 0:2], par[2:3, 0:1], par[3:4, 0:1]], axis=1)
    log_decay = -jnp.exp(a_log) * softplus(gt + dt_bias)
    bf_s[...] = jnp.broadcast_to(beta[:, 0:1], shape)
    bb_s[...] = jnp.broadcast_to(beta[:, 1:2], shape)
    gf_s[...] = jnp.broadcast_to(log_decay[:, 2:3], shape)
    gb_s[...] = jnp.broadcast_to(log_decay[:, 3:4], shape)

    pos = lax.broadcasted_iota(jnp.int32, shape, 0) & (CHUNK - 1)
    pad_s[0:off, :] = jnp.zeros((off, LANES), F32)
    pad_s[off + ltot:off + ltot + off, :] = jnp.zeros((off, LANES), F32)
    step = 1
    while step < CHUNK:
        pad_s[off:off + ltot, :] = gf_s[...]
        gf_s[...] = gf_s[...] + jnp.where(pos >= step, pad_s[off - step:off - step + ltot, :], 0.0)
        pad_s[off:off + ltot, :] = gb_s[...]
        gb_s[...] = gb_s[...] + jnp.where(pos < CHUNK - step, pad_s[off + step:off + step + ltot, :], 0.0)
        step *= 2

    ri = lax.broadcasted_iota(jnp.int32, (CHUNK, CHUNK), 0)
    ci = lax.broadcasted_iota(jnp.int32, (CHUNK, CHUNK), 1)
    eye = (ri == ci).astype(F32)
    hd = DN_HEAD_DIM
    pair_mask = {1 << b: jnp.logical_and((ri >> (b + 1)) == (ci >> (b + 1)), (ri >> b) != (ci >> b))
                 for b in range(6)}

    group = max(g for g in range(1, 10) if n_tot % g == 0)

    def stage1(g, carry):
        chains = [(g * group + c, d) for c in range(group) for d in (0, 1)]
        both = lambda fn: [fn(n, d) for (n, d) in chains]
        each = lambda fn, *cols: [fn(*vals) for vals in zip(*cols)]
        rows = lambda n: pl.ds(pl.multiple_of(n * CHUNK, CHUNK), CHUNK)
        incl = both(lambda n, d: (ri >= ci) if d == 0 else (ri <= ci))
        strict = both(lambda n, d: (ri > ci) if d == 0 else (ri < ci))
        q = both(lambda n, d: q_s[rows(n), :])
        k = both(lambda n, d: k_s[rows(n), :])
        v = both(lambda n, d: v_s[rows(n), :])
        gi = both(lambda n, d: (gf_s if d == 0 else gb_s)[rows(n), :])
        bi = both(lambda n, d: (bf_s if d == 0 else bb_s)[rows(n), :])
        diff = each(lambda g_: g_[:, 0:CHUNK] - g_.T[0:CHUNK, :], gi)
        decay = each(lambda m, df: jnp.where(m, jnp.exp(jnp.where(m, df, 0.0)), 0.0), incl, diff)
        kb = each(lambda k_, b_: k_ * b_, k, bi)
        kk = each(_mm_nt, kb, k)
        low = each(lambda m, kk_, dec: jnp.where(m, kk_ * dec, 0.0), strict, kk, decay)
        inv = [eye - jnp.where(pair_mask[1], l_, 0.0) for l_ in low]
        size = 2
        while size < CHUNK:
            coupling = [jnp.where(pair_mask[size], l_, 0.0) for l_ in low]
            left = each(lambda a_, b_: _mm_inv(a_, b_), inv, coupling)
            corr = each(lambda a_, b_: _mm_inv(a_, b_), left, inv)
            inv = each(lambda t, u_: t - u_, inv, corr)
            size *= 2
        eg = [jnp.exp(g_) for g_ in gi]
        rhs = each(lambda kb_, e_, v_, b_: jnp.concatenate([kb_ * e_, v_ * b_], axis=1), kb, eg, v, bi)
        sol = each(_mm, inv, rhs)
        qk = each(_mm_nt, q, k)
        intra = each(lambda m, qk_, dec: jnp.where(m, qk_ * dec, 0.0), incl, qk, decay)
        g_last = [g_[CHUNK - 1:CHUNK, :] if d == 0 else g_[0:1, :] for g_, (_, d) in zip(gi, chains)]
        k_tail_t = each(lambda k_, gl, g_: (k_ * jnp.exp(gl - g_)).T, k, g_last, gi)
        iwu = each(_mm, intra, sol)
        kwu = each(_mm, k_tail_t, sol)
        for idx, (n, d) in enumerate(chains):
            sk = pl.ds(pl.multiple_of(n * hd, hd), hd)
            qe_s[d, rows(n), :] = (q[idx] * eg[idx] - iwu[idx][:, 0:hd]).astype(BF16)
            o0_s[d, rows(n), :] = iwu[idx][:, hd:]
            kw_s[d, sk, :] = kwu[idx][:, 0:hd].astype(BF16)
            c_s[d, sk, :] = kwu[idx][:, hd:]
            dc_s[d, pl.ds(pl.multiple_of(n * 8, 8), 8), :] = jnp.broadcast_to(jnp.exp(g_last[idx]), (8, LANES))
        return carry

    lax.fori_loop(0, n_tot // group, stage1, 0)

    def advance(direction, n, state):
        sl = pl.ds(pl.multiple_of(n * CHUNK, CHUNK), CHUNK)
        sk = pl.ds(pl.multiple_of(n * hd, hd), hd)
        sb = state.astype(BF16)
        o_s[direction, sl, :] = (jnp.dot(qe_s[direction, sl, :], sb, preferred_element_type=F32)
                                 + o0_s[direction, sl, :])
        dc = dc_s[direction, pl.ds(pl.multiple_of(n * 8, 8), 1), :]
        return (state * dc - jnp.dot(kw_s[direction, sk, :], sb, preferred_element_type=F32)
                + c_s[direction, sk, :])

    def stage2(t, states):
        n_b = jnp.where(t < n_ctx, n_ctx - 1 - t, n_tot + n_ctx - 1 - t)
        return advance(0, t, states[0]), advance(1, n_b, states[1])

    zero = jnp.zeros((hd, hd), F32)
    lax.fori_loop(0, n_tot, stage2, (zero, zero))

    o = o_s[0] + o_s[1]
    o = o * lax.rsqrt(jnp.mean(o * o, axis=-1, keepdims=True) + RMS_EPS) * dnw_ref[...]
    oz_ref[0] = o[0:m_len, :] * _silu(zz_ref[0])
    ox_ref[0] = o[m_len:, :] * _silu(zx_ref[0])


def _delta(pz, px, gates, conv_w, par, dn_norm_w):
    b, m_len, _ = pz.shape
    l_len = px.shape[1]
    ltot = m_len + l_len
    n_tot = ltot // CHUNK
    hb = LANES

    def col(c0, length):
        return pl.BlockSpec((1, length, hb), lambda i, h: (i, 0, c0 // hb + h))

    scratch = [pltpu.VMEM((ltot, LANES), F32) for _ in range(7)]
    scratch += [pltpu.VMEM((2, ltot, LANES), F32),
                pltpu.VMEM((ltot + 2 * CHUNK, LANES), F32),
                pltpu.VMEM((2, n_tot * DN_HEAD_DIM, LANES), BF16),
                pltpu.VMEM((2, n_tot * DN_HEAD_DIM, LANES), F32),
                pltpu.VMEM((2, ltot, LANES), BF16),
                pltpu.VMEM((2, ltot, LANES), F32),
                pltpu.VMEM((2, n_tot * 8, LANES), F32)]
    return pl.pallas_call(
        functools.partial(_delta_kernel, m_len=m_len, l_len=l_len),
        grid=(b, DN_HEADS),
        in_specs=[col(COL_Q, m_len), col(COL_K, m_len), col(COL_V, m_len), col(COL_Z, m_len),
                  col(COL_Q, l_len), col(COL_K, l_len), col(COL_V, l_len), col(COL_Z, l_len),
                  pl.BlockSpec((1, 1, ltot, 4), lambda i, h: (i, h, 0, 0)),
                  pl.BlockSpec((1, 3, CONV_WIDTH, hb), lambda i, h: (h, 0, 0, 0)),
                  pl.BlockSpec((1, 8, hb), lambda i, h: (h, 0, 0)),
                  pl.BlockSpec((1, hb), lambda i, h: (0, 0))],
        out_specs=[pl.BlockSpec((1, l_len, hb), lambda i, h: (i, 0, h)),
                   pl.BlockSpec((1, m_len, hb), lambda i, h: (i, 0, h))],
        out_shape=[jax.ShapeDtypeStruct((b, l_len, DN_WIDTH), F32),
                   jax.ShapeDtypeStruct((b, m_len, DN_WIDTH), F32)],
        scratch_shapes=scratch,
        compiler_params=_params(("parallel", "parallel")),
        name="delta",
    )(pz, pz, pz, pz, px, px, px, px, gates, conv_w, par, dn_norm_w)


def _fnet_weight_kernel(c_ref, s_ref, w_ref, o_ref, *, scale):
    w = w_ref[...]
    o_ref[:, 0:FNET_WIDTH] = _mm_f32(c_ref[...], w) * scale
    o_ref[:, FNET_WIDTH:] = _mm_f32(s_ref[...], w) * scale


def _fnet_weights(w_bd, seq_len):
    n = FNET_GROUP_DIM
    ang = 2.0 * np.pi * ((np.arange(n)[:, None] * np.arange(n)[None, :]) % n) / n
    eye4 = np.eye(FNET_WIDTH // n)
    c_bd = jnp.asarray(np.kron(eye4, np.cos(ang)), F32)
    s_bd = jnp.asarray(np.kron(eye4, np.sin(ang)), F32)
    return pl.pallas_call(
        functools.partial(_fnet_weight_kernel, scale=float(1.0 / math.sqrt(seq_len * n))),
        out_shape=jax.ShapeDtypeStruct((FNET_WIDTH, 2 * FNET_WIDTH), F32),
        name="fnet_weights",
    )(c_bd, s_bd, w_bd)


def _fnet_kernel(x_ref, w_ref, c_ref, s_ref, o_ref):
    xw = _mm(x_ref[0], w_ref[...])
    o_ref[0] = _mm(c_ref[...], xw[:, 0:FNET_WIDTH]) + _mm(s_ref[...], xw[:, FNET_WIDTH:])


def _dft_mats(seq_len):
    idx = (np.arange(seq_len)[:, None] * np.arange(seq_len)[None, :]) % seq_len
    ang = 2.0 * np.pi * idx / seq_len
    return jnp.asarray(np.cos(ang), F32).astype(BF16), jnp.asarray(-np.sin(ang), F32).astype(BF16)


def _fnet(p, wcs):
    b, l, _ = p.shape
    tm = min(512, l)
    cm, sm = _dft_mats(l)
    return pl.pallas_call(
        _fnet_kernel,
        grid=(l // tm, b),
        in_specs=[pl.BlockSpec((1, l, FNET_WIDTH), lambda i, j: (j, 0, COL_FNET // FNET_WIDTH)),
                  pl.BlockSpec((FNET_WIDTH, 2 * FNET_WIDTH), lambda i, j: (0, 0)),
                  pl.BlockSpec((tm, l), lambda i, j: (i, 0)),
                  pl.BlockSpec((tm, l), lambda i, j: (i, 0))],
        out_specs=pl.BlockSpec((1, tm, FNET_WIDTH), lambda i, j: (j, i, 0)),
        out_shape=jax.ShapeDtypeStruct((b, l, FNET_WIDTH), F32),
        compiler_params=_params(("parallel", "parallel")),
        name="fnet",
    )(p, wcs, cm, sm)


def _mixout_kernel(x_ref, dn_ref, p_ref, f_ref, band_ref, icnt_ref, wp_ref, ps_ref, wo_ref,
                   mod_ref, o_ref):
    xp = p_ref[0]
    group = lax.broadcasted_iota(jnp.int32, xp.shape, 1) // POOL_GROUP_DIM
    win_sum = jnp.zeros(xp.shape, F32)
    for g in range(len(POOL_WINDOWS)):
        win_sum = win_sum + _mm(band_ref[g], jnp.where(group == g, xp, 0.0))
    y = win_sum * icnt_ref[...] - xp
    pool = _mm(y, wp_ref[...]) * ps_ref[...]
    out = (_mm(dn_ref[0], wo_ref[0:DN_WIDTH, :])
           + _mm(pool, wo_ref[DN_WIDTH:DN_WIDTH + POOL_WIDTH, :])
           + _mm(f_ref[0], wo_ref[DN_WIDTH + POOL_WIDTH:, :]))
    o_ref[0] = x_ref[0] + mod_ref[0, 2:3, :] * out


def _pool_consts(tm, seg):
    pos = np.arange(tm) % seg
    seg_id = np.arange(tm) // seg
    band = np.zeros((len(POOL_WINDOWS), tm, tm), np.float32)
    icnt = np.zeros((tm, POOL_WIDTH), np.float32)
    for g, w in enumerate(POOL_WINDOWS):
        lo = np.clip(pos - w // 2, 0, seg)
        hi = np.clip(pos + w - w // 2, 0, seg)
        j = np.arange(tm)
        inside = (seg_id[:, None] == seg_id[None, :]) & (pos[None, :] >= lo[:, None]) & (pos[None, :] < hi[:, None])
        band[g] = inside.astype(np.float32)
        icnt[:, g * POOL_GROUP_DIM:(g + 1) * POOL_GROUP_DIM] = (1.0 / (hi - lo))[:, None]
    return jnp.asarray(band).astype(BF16), jnp.asarray(icnt)


def _mixout(x, dn, p, fn, seg, wp_bd, pool_scale, w_out, mods):
    b, l, d = x.shape
    tm = min(256, l)
    band, icnt = _pool_consts(tm, min(seg, tm))
    if seg > tm:
        raise ValueError("pooling segment longer than the token tile")
    return pl.pallas_call(
        _mixout_kernel,
        grid=(b, l // tm),
        in_specs=[pl.BlockSpec((1, tm, d), lambda i, j: (i, j, 0)),
                  pl.BlockSpec((1, tm, DN_WIDTH), lambda i, j: (i, j, 0)),
                  pl.BlockSpec((1, tm, POOL_WIDTH), lambda i, j: (i, j, COL_POOL // POOL_WIDTH)),
                  pl.BlockSpec((1, tm, FNET_WIDTH), lambda i, j: (i, j, 0)),
                  pl.BlockSpec((len(POOL_WINDOWS), tm, tm), lambda i, j: (0, 0, 0)),
                  pl.BlockSpec((tm, POOL_WIDTH), lambda i, j: (0, 0)),
                  pl.BlockSpec((POOL_WIDTH, POOL_WIDTH), lambda i, j: (0, 0)),
                  pl.BlockSpec((1, POOL_WIDTH), lambda i, j: (0, 0)),
                  pl.BlockSpec((d, d), lambda i, j: (0, 0)),
                  pl.BlockSpec((1, 8, d), lambda i, j: (i, 0, 0))],
        out_specs=pl.BlockSpec((1, tm, d), lambda i, j: (i, j, 0)),
        out_shape=jax.ShapeDtypeStruct((b, l, d), F32),
        compiler_params=_params(("parallel", "parallel")),
        name="mixout",
    )(x, dn, p, fn, band, icnt, wp_bd, pool_scale, w_out, mods)


def _peer_kernel(x_ref, mod_ref, nw_ref, wq_ref, keys_ref, u_ref, vt_ref, o_ref,
                 ht_s, c1_s, e1_s, r2_s, e2_s, acc_s, act_s, acta_s, wga_s, wgb_s, sc_s, rc_s, re_s,
                 *, n_chunks, blocks_per_half):
    p = pl.program_id(2)
    neg_inf = float("-inf")
    half = blocks_per_half * N_KEYS
    n_lane_groups = x_ref.shape[1] // LANES
    d_model = x_ref.shape[2]

    @pl.when(p == 0)
    def _route():
        h = _norm_mod(x_ref[0], nw_ref[...], mod_ref[0, 3:4, :], mod_ref[0, 4:5, :])
        ht_s[...] = h.T.astype(BF16)

        marker = 2.0 ** 100

        def extract(s):
            vals = []
            for r in range(PEER_TOPK):
                mx = jnp.max(s, axis=0, keepdims=True)
                vals.append(mx)
                s = jnp.where(s == mx, -marker * (r + 1), s)
            rank = jnp.where(s < -0.5 * marker, s * (-1.0 / marker) - 1.0, 99.0)
            return vals, rank

        pairs = [(a, b) for a in range(PEER_TOPK) for b in range(PEER_TOPK)
                 if (a + 1) * (b + 1) <= PEER_TOPK]
        n_pad = (-len(pairs)) % 8

        for hd in range(PEER_HEADS):
            qt = jnp.dot(wq_ref[hd * 2 * N_KEYS:(hd + 1) * 2 * N_KEYS, :], ht_s[...],
                         preferred_element_type=F32)
            sc_s[0] = _mm(keys_ref[hd, 0], qt[0:N_KEYS, :])
            sc_s[1] = _mm(keys_ref[hd, 1], qt[N_KEYS:, :])
            for tg in range(n_lane_groups):
                ln = slice(tg * LANES, (tg + 1) * LANES)
                s1 = sc_s[0, :, ln]
                s2 = sc_s[1, :, ln]
                t1, rank1 = extract(s1)
                t2, rank2 = extract(s2)
                cands = [t1[a] + t2[b] for a, b in pairs]
                cand = jnp.concatenate(cands + [jnp.full_like(cands[0], neg_inf)] * n_pad, axis=0)
                work = cand
                thr = jnp.full_like(cands[0], neg_inf)
                found = jnp.zeros_like(cands[0])
                for _ in range(PEER_TOPK):
                    mx = jnp.max(work, axis=0, keepdims=True)
                    cnt = jnp.sum(jnp.where(cand >= mx, 1.0, 0.0), axis=0, keepdims=True)
                    hit = jnp.where(cnt >= PEER_TOPK, 1.0 - found, 0.0)
                    thr = jnp.where(hit > 0.0, mx, thr)
                    found = jnp.maximum(found, hit)
                    work = jnp.where(work == mx, neg_inf, work)
                z = jnp.sum(jnp.where(cand >= thr, jnp.exp(cand - cands[0]), 0.0), axis=0, keepdims=True)
                cnt1 = jnp.zeros(s1.shape, F32)
                for a in range(PEER_TOPK):
                    partners = jnp.zeros_like(thr)
                    for idx, (pa, _) in enumerate(pairs):
                        if pa == a:
                            partners = partners + jnp.where(cands[idx] >= thr, 1.0, 0.0)
                    cnt1 = jnp.where(rank1 == float(a), partners, cnt1)
                c1_s[hd, :, ln] = cnt1
                e1_s[hd, :, ln] = jnp.exp(s1 - t1[0]) * (1.0 / z)
                r2_s[hd, :, ln] = rank2.astype(BF16)
                e2_s[hd, :, ln] = jnp.exp(s2 - t2[0]).astype(BF16)
        acc_s[...] = jnp.zeros(acc_s.shape, F32)
        act_s[...] = jnp.zeros(act_s.shape, F32)
        wga_s[...] = jnp.zeros(wga_s.shape, BF16)

    def row_bf16(row):
        tile = jnp.broadcast_to(row, (16, LANES)).astype(BF16)
        return jnp.concatenate([tile] * (N_KEYS // 16), axis=0)

    def stage_rows(first_block, slot):
        for ii in range(blocks_per_half):
            i = jnp.maximum(first_block + ii, 0)
            for hd in range(PEER_HEADS):
                k = ii * PEER_HEADS + hd
                rc_s[slot, k:k + 1, :] = c1_s[hd, pl.ds(i, 1), :]
                re_s[slot, k:k + 1, :] = e1_s[hd, pl.ds(i, 1), :]

    def gate_unit(act, slot, out_ref, tg, g0):
        ln = slice(tg * LANES, (tg + 1) * LANES)
        blocks = list(range(g0, min(g0 + 4, blocks_per_half)))
        gates = [jnp.zeros((N_KEYS, LANES), BF16) for _ in blocks]
        for hd in range(PEER_HEADS):
            r2 = r2_s[hd, :, ln]
            e2 = e2_s[hd, :, ln]
            for n, ii in enumerate(blocks):
                k = ii * PEER_HEADS + hd
                c1row = row_bf16(rc_s[slot, k:k + 1, ln])
                e1row = row_bf16(re_s[slot, k:k + 1, ln])
                gates[n] = gates[n] + jnp.minimum(jnp.maximum(c1row - r2, 0.0), e1row) * e2
        for n, ii in enumerate(blocks):
            a = act[ii * N_KEYS:(ii + 1) * N_KEYS, ln]
            gelu = 0.5 * a * (1.0 + lax.erf(a * (2.0 ** -0.5)))
            out_ref[ii * N_KEYS:(ii + 1) * N_KEYS, ln] = gates[n] * gelu.astype(BF16)

    def phase(matmul_piece, act, slot, out_ref, lane_groups):
        units = [(tg, g0) for tg in lane_groups for g0 in range(0, blocks_per_half, 4)]
        matmul_piece(0, 1)
        for unit in units:
            gate_unit(act, slot, out_ref, *unit)

    def value_piece(lo, w_ref):
        def piece(idx, n_pieces):
            rows = d_model // n_pieces
            r0 = idx * rows
            acc_s[r0:r0 + rows, :] += jnp.dot(vt_ref[r0:r0 + rows, lo:lo + half], w_ref[...],
                                              preferred_element_type=F32)
        return piece

    def act_piece(lo, dst_ref):
        def piece(idx, n_pieces):
            rows = half // n_pieces
            r0 = idx * rows
            dst_ref[r0:r0 + rows, :] = jnp.dot(u_ref[lo + r0:lo + r0 + rows, :], ht_s[...],
                                               preferred_element_type=F32)
        return piece

    prev_b = (2 * p - 1) * blocks_per_half

    first = tuple(range(n_lane_groups // 2))
    second = tuple(range(n_lane_groups // 2, n_lane_groups))

    @pl.when(p < n_chunks)
    def _stage():
        stage_rows(prev_b, 0)
        stage_rows(2 * p * blocks_per_half, 1)

    @pl.when(p + 1 <= n_chunks)
    def _phase1():
        phase(value_piece(0, wga_s), act_s, 0, wgb_s, first)

    @pl.when(p + 2 <= n_chunks + 1)
    def _phase2():
        phase(act_piece(0, acta_s), act_s, 0, wgb_s, second)

    @pl.when(p + 3 <= n_chunks + 2)
    def _phase3():
        phase(value_piece(half, wgb_s), acta_s, 1, wga_s, first)

    @pl.when(p + 4 <= n_chunks + 3)
    def _phase4():
        phase(act_piece(half, act_s), acta_s, 1, wga_s, second)

    @pl.when(p == n_chunks)
    def _drain1():
        stage_rows(prev_b, 0)
        phase(value_piece(0, wga_s), act_s, 0, wgb_s, first + second)

    @pl.when(p >= n_chunks)
    def _drain2():
        out = acc_s[...] + jnp.dot(vt_ref[:, half:], wgb_s[...], preferred_element_type=F32)
        o_ref[0] = x_ref[0] + mod_ref[0, 5:6, :] * out.T


def _peer(x, mods, nw, wq_t, keys, u, v_t, tm=512, chunk=2048):
    b, l, d = x.shape
    tm = min(tm, l)
    n_chunks = N_EXPERTS // chunk
    half = chunk // 2
    kern = functools.partial(_peer_kernel, n_chunks=n_chunks, blocks_per_half=half // N_KEYS)
    head_shape = (PEER_HEADS, N_KEYS, tm)
    once = pl.Buffered(1)
    return pl.pallas_call(
        kern,
        grid=(b, l // tm, n_chunks + 1),
        in_specs=[pl.BlockSpec((1, tm, d), lambda i, t, j: (i, t, 0)),
                  pl.BlockSpec((1, 8, d), lambda i, t, j: (i, 0, 0)),
                  pl.BlockSpec((1, d), lambda i, t, j: (0, 0)),
                  pl.BlockSpec(wq_t.shape, lambda i, t, j: (0, 0), pipeline_mode=once),
                  pl.BlockSpec(keys.shape, lambda i, t, j: (0, 0, 0, 0), pipeline_mode=once),
                  pl.BlockSpec((chunk, d), lambda i, t, j: (jnp.minimum(j, n_chunks - 1), 0)),
                  pl.BlockSpec((d, chunk), lambda i, t, j: (0, jnp.maximum(j - 1, 0)))],
        out_specs=pl.BlockSpec((1, tm, d), lambda i, t, j: (i, t, 0)),
        out_shape=jax.ShapeDtypeStruct((b, l, d), F32),
        scratch_shapes=[pltpu.VMEM((d, tm), BF16),
                        pltpu.VMEM(head_shape, F32), pltpu.VMEM(head_shape, F32),
                        pltpu.VMEM(head_shape, BF16), pltpu.VMEM(head_shape, BF16),
                        pltpu.VMEM((d, tm), F32),
                        pltpu.VMEM((half, tm), F32), pltpu.VMEM((half, tm), F32),
                        pltpu.VMEM((half, tm), BF16), pltpu.VMEM((half, tm), BF16),
                        pltpu.VMEM((2, N_KEYS, tm), F32),
                        pltpu.VMEM((2, half // N_KEYS * PEER_HEADS, tm), F32),
                        pltpu.VMEM((2, half // N_KEYS * PEER_HEADS, tm), F32)],
        compiler_params=_params(("parallel", "parallel", "arbitrary")),
        name="peer",
    )(x, mods, nw, wq_t, keys, u, v_t)


def _final_norm_kernel(x_ref, w_ref, o_ref):
    x = x_ref[0]
    o_ref[0] = x * lax.rsqrt(jnp.mean(x * x, axis=-1, keepdims=True) + RMS_EPS) * w_ref[...]


def _final_norm(x, w):
    b, l, d = x.shape
    tm = min(512, l)
    return pl.pallas_call(
        _final_norm_kernel,
        grid=(b, l // tm),
        in_specs=[pl.BlockSpec((1, tm, d), lambda i, j: (i, j, 0)),
                  pl.BlockSpec((1, d), lambda i, j: (0, 0))],
        out_specs=pl.BlockSpec((1, tm, d), lambda i, j: (i, j, 0)),
        out_shape=jax.ShapeDtypeStruct((b, l, d), F32),
        compiler_params=_params(("parallel", "parallel")),
        name="final_norm",
    )(x, w)


def _block_diag(w):
    g, n, _ = w.shape
    out = jnp.zeros((g * n, g * n), w.dtype)
    for i in range(g):
        out = out.at[i * n:(i + 1) * n, i * n:(i + 1) * n].set(w[i])
    return out


def _reorder_w_in(w):
    d = w.shape[0]
    return jnp.concatenate([w[:, :REF_QKV], w[:, REF_ALPHA:REF_Z], w[:, REF_Z:REF_POOL],
                            w[:, REF_POOL:REF_FNET], w[:, REF_QKV:REF_ALPHA],
                            jnp.zeros((d, IN_PAD - REF_FNET), w.dtype)], axis=1).astype(BF16)


def _gate_columns(pz, px):
    ba = jnp.concatenate([pz[:, :, COL_BA:COL_BA + 16], px[:, :, COL_BA:COL_BA + 16]], axis=1)
    b, ltot, _ = ba.shape
    return ba.reshape(b, ltot, 4, DN_HEADS).transpose(0, 3, 1, 2)


def _mod_rows(mod, rows):
    d = mod.shape[1] // N_MOD
    m = mod.reshape(16, N_MOD, d)[rows]
    return jnp.concatenate([m, jnp.zeros((m.shape[0], 8 - N_MOD, d), m.dtype)], axis=1)


def kernel(x, c, ctx, c_ctx, w_mod, b_mod, norm1_w, norm2_w, w_in, conv_w, a_log, dt_bias,
           dn_norm_w, w_pool, pool_scale, w_fnet, w_out, w_query, sub_keys, expert_u, expert_v,
           final_norm_w):
    b, l, d = x.shape
    m = ctx.shape[1]
    depth = w_mod.shape[0]
    rows = l // GRID_W

    cs = jnp.concatenate([c, c_ctx[None, :], jnp.zeros((16 - b - 1, d), F32)], axis=0)
    mod_all = _modulation(cs, w_mod, b_mod)

    z = ctx
    for i in range(depth):
        update_ctx = i < depth - 1
        mods_x = _mod_rows(mod_all[i], jnp.arange(b))
        mods_z = _mod_rows(mod_all[i], jnp.full((b,), b))
        w_in_r = _reorder_w_in(w_in[i])
        nw1 = norm1_w[i][None, :]
        nw2 = norm2_w[i][None, :]

        px = _inproj(x, mods_x, nw1, w_in_r)
        pz = _inproj(z, mods_z, nw1, w_in_r)

        cw = conv_w[i].reshape(CONV_WIDTH, 3, DN_HEADS, DN_HEAD_DIM).transpose(2, 1, 0, 3)
        par = jnp.concatenate([a_log[i], dt_bias[i]], axis=0)
        par = jnp.broadcast_to(par.T[:, :, None], (DN_HEADS, 4, LANES))
        par = jnp.concatenate([par, jnp.zeros((DN_HEADS, 4, LANES), F32)], axis=1)
        dn_x, dn_z = _delta(pz, px, _gate_columns(pz, px), cw, par, dn_norm_w[i][None, :])

        wp_bd = _block_diag(w_pool[i]).astype(BF16)
        wf_bd = _block_diag(w_fnet[i])
        w_out_b = w_out[i].astype(BF16)
        ps = pool_scale[i][None, :]

        fn_x = _fnet(px, _fnet_weights(wf_bd, l).astype(BF16))
        x = _mixout(x, dn_x, px, fn_x, l // rows, wp_bd, ps, w_out_b, mods_x)
        if update_ctx:
            fn_z = _fnet(pz, _fnet_weights(wf_bd, m).astype(BF16))
            z = _mixout(z, dn_z, pz, fn_z, m, wp_bd, ps, w_out_b, mods_z)

        wq_t = w_query[i].T.astype(BF16)
        keys = sub_keys[i].astype(BF16)
        u_b = expert_u[i].astype(BF16)
        v_t = expert_v[i].T.astype(BF16)
        x = _peer(x, mods_x, nw2, wq_t, keys, u_b, v_t)
        if update_ctx:
            z = _peer(z, mods_z, nw2, wq_t, keys, u_b, v_t)

    return _final_norm(x, final_norm_w[None, :])
```

```python
import functools
import math

import numpy as np
import jax
import jax.numpy as jnp
from jax import lax
from jax.experimental import pallas as pl
from jax.experimental.pallas import tpu as pltpu

F32 = jnp.float32
BF16 = jnp.bfloat16
HIGHEST = lax.Precision.HIGHEST

D_MODEL = 1024
DN_HEADS = 4
DN_HEAD_DIM = 128
DN_WIDTH = DN_HEADS * DN_HEAD_DIM
POOL_WINDOWS = (2, 4, 8, 16)
POOL_WIDTH = 256
POOL_GROUP_DIM = 64
FNET_WIDTH = 256
FNET_GROUP_DIM = 64
CONV_WIDTH = 4
CHUNK = 64
GRID_W = 64
PEER_HEADS = 8
N_KEYS = 128
N_EXPERTS = N_KEYS * N_KEYS
PEER_TOPK = 16
N_MOD = 6
RMS_EPS = 1e-6
L2_EPS = 1e-6

COL_Q, COL_K, COL_V, COL_Z = 0, 512, 1024, 1536
COL_POOL, COL_FNET, COL_BA = 2048, 2304, 2560
IN_PAD = 2688
REF_QKV, REF_BETA, REF_ALPHA, REF_Z, REF_POOL, REF_FNET = 1536, 1544, 1552, 2064, 2320, 2576

LANES = 128
VMEM_LIMIT = 56 * 1024 * 1024


def _params(sem, flags=None):
    return pltpu.CompilerParams(dimension_semantics=sem, vmem_limit_bytes=VMEM_LIMIT, flags=flags)


def _mm(a, b):
    return jnp.dot(a.astype(BF16), b.astype(BF16), preferred_element_type=F32)


def _mm_nt(a, b):
    return lax.dot_general(a.astype(BF16), b.astype(BF16), (((1,), (1,)), ((), ())),
                           preferred_element_type=F32)


def _mm_inv(a, b):
    return _mm(a, b)


def _mm_f32(a, b):
    return jnp.dot(a, b, precision=HIGHEST, preferred_element_type=F32)


def _silu(x):
    return x * jax.nn.sigmoid(x)


def _mod_kernel(c_ref, w_ref, b_ref, o_ref):
    o_ref[0] = _mm_f32(_silu(c_ref[...]), w_ref[0]) + b_ref[0]


def _modulation(cs, w_mod, b_mod):
    depth, d, n = w_mod.shape
    tn = 1536
    return pl.pallas_call(
        _mod_kernel,
        grid=(depth, n // tn),
        in_specs=[pl.BlockSpec((16, d), lambda i, j: (0, 0)),
                  pl.BlockSpec((1, d, tn), lambda i, j: (i, 0, j)),
                  pl.BlockSpec((1, 1, tn), lambda i, j: (i, 0, j))],
        out_specs=pl.BlockSpec((1, 16, tn), lambda i, j: (i, 0, j)),
        out_shape=jax.ShapeDtypeStruct((depth, 16, n), F32),
        compiler_params=_params(("parallel", "parallel")),
        name="modulation",
    )(cs, w_mod, b_mod.reshape(depth, 1, n))


def _norm_mod(x, nw, shift, scale):
    y = x * lax.rsqrt(jnp.mean(x * x, axis=-1, keepdims=True) + RMS_EPS) * nw
    return y * (1.0 + scale) + shift


def _inproj_kernel(x_ref, mod_ref, nw_ref, w_ref, o_ref, g_ref):
    h = _norm_mod(x_ref[0], nw_ref[...], mod_ref[0, 0:1, :], mod_ref[0, 1:2, :])
    p = _mm(h, w_ref[...])
    o_ref[0] = p
    g_ref[0] = p[:, COL_BA:]


def _inproj(x, mods, nw, w):
    b, l, d = x.shape
    n = w.shape[1]
    tm = min(512, l)
    return pl.pallas_call(
        _inproj_kernel,
        grid=(b, l // tm),
        in_specs=[pl.BlockSpec((1, tm, d), lambda i, j: (i, j, 0)),
                  pl.BlockSpec((1, 8, d), lambda i, j: (i, 0, 0)),
                  pl.BlockSpec((1, d), lambda i, j: (0, 0)),
                  pl.BlockSpec((d, n), lambda i, j: (0, 0))],
        out_specs=[pl.BlockSpec((1, tm, n), lambda i, j: (i, j, 0)),
                   pl.BlockSpec((1, tm, n - COL_BA), lambda i, j: (i, j, 0))],
        out_shape=[jax.ShapeDtypeStruct((b, l, n), F32),
                   jax.ShapeDtypeStruct((b, l, n - COL_BA), F32)],
        compiler_params=_params(("parallel", "parallel")),
        name="inproj",
    )(x, mods, nw, w)


def _delta_kernel(qz_ref, kz_ref, vz_ref, zz_ref, qx_ref, kx_ref, vx_ref, zx_ref,
                  gate_ref, cw_ref, par_ref, dnw_ref, ox_ref, oz_ref,
                  q_s, k_s, v_s, gf_s, gb_s, bf_s, bb_s, o_s, pad_s, kw_s, c_s, qe_s, o0_s, dc_s,
                  *, m_len, l_len):
    ltot = m_len + l_len
    n_ctx = m_len // CHUNK
    n_tot = ltot // CHUNK
    off = CHUNK

    def conv_silu(src_ref, w, length):
        pad_s[off - 8:off, :] = jnp.zeros((8, LANES), F32)
        pad_s[off:off + length, :] = src_ref[0]
        pad_s[off + length:off + length + 8, :] = jnp.zeros((8, LANES), F32)
        y = (w[0:1, :] * pad_s[off - 2:off - 2 + length, :]
             + w[1:2, :] * pad_s[off - 1:off - 1 + length, :]
             + w[2:3, :] * pad_s[off:off + length, :]
             + w[3:4, :] * pad_s[off + 1:off + 1 + length, :])
        return _silu(y)

    def l2norm(t):
        return t * lax.rsqrt(jnp.sum(t * t, axis=-1, keepdims=True) + L2_EPS)

    for (src_q, src_k, src_v, start, length) in ((qz_ref, kz_ref, vz_ref, 0, m_len),
                                                 (qx_ref, kx_ref, vx_ref, m_len, l_len)):
        q_s[start:start + length, :] = l2norm(conv_silu(src_q, cw_ref[0, 0], length)) * (DN_HEAD_DIM ** -0.5)
        k_s[start:start + length, :] = l2norm(conv_silu(src_k, cw_ref[0, 1], length))
        v_s[start:start + length, :] = conv_silu(src_v, cw_ref[0, 2], length)

    gt = gate_ref[0, 0]
    par = par_ref[0]
    shape = (ltot, LANES)
    def softplus(t):
        return jnp.maximum(t, 0.0) + jnp.log1p(jnp.exp(-jnp.abs(t)))

    beta = jax.nn.sigmoid(gt)
    a_log = jnp.concatenate([par[0:1, 0:2], par[0:1, 0:1], par[1:2, 0:1]], axis=1)
    dt_bias = jnp.concatenate([par[0:1, 0:2], par[2:3, 0:1], par[3:4, 0:1]], axis=1)
    log_decay = -jnp.exp(a_log) * softplus(gt + dt_bias)
    bf_s[...] = jnp.broadcast_to(beta[:, 0:1], shape)
    bb_s[...] = jnp.broadcast_to(beta[:, 1:2], shape)
    gf_s[...] = jnp.broadcast_to(log_decay[:, 2:3], shape)
    gb_s[...] = jnp.broadcast_to(log_decay[:, 3:4], shape)

    pos = lax.broadcasted_iota(jnp.int32, shape, 0) & (CHUNK - 1)
    pad_s[0:off, :] = jnp.zeros((off, LANES), F32)
    pad_s[off + ltot:off + ltot + off, :] = jnp.zeros((off, LANES), F32)
    step = 1
    while step < CHUNK:
        pad_s[off:off + ltot, :] = gf_s[...]
        gf_s[...] = gf_s[...] + jnp.where(pos >= step, pad_s[off - step:off - step + ltot, :], 0.0)
        pad_s[off:off + ltot, :] = gb_s[...]
        gb_s[...] = gb_s[...] + jnp.where(pos < CHUNK - step, pad_s[off + step:off + step + ltot, :], 0.0)
        step *= 2

    ri = lax.broadcasted_iota(jnp.int32, (CHUNK, CHUNK), 0)
    ci = lax.broadcasted_iota(jnp.int32, (CHUNK, CHUNK), 1)
    eye = (ri == ci).astype(F32)
    hd = DN_HEAD_DIM
    pair_mask = {1 << b: jnp.logical_and((ri >> (b + 1)) == (ci >> (b + 1)), (ri >> b) != (ci >> b))
                 for b in range(6)}

    group = max(g for g in range(1, 10) if n_tot % g == 0)

    def stage1(g, carry):
        chains = [(g * group + c, d) for c in range(group) for d in (0, 1)]
        both = lambda fn: [fn(n, d) for (n, d) in chains]
        each = lambda fn, *cols: [fn(*vals) for vals in zip(*cols)]
        rows = lambda n: pl.ds(pl.multiple_of(n * CHUNK, CHUNK), CHUNK)
        incl = both(lambda n, d: (ri >= ci) if d == 0 else (ri <= ci))
        strict = both(lambda n, d: (ri > ci) if d == 0 else (ri < ci))
        q = both(lambda n, d: q_s[rows(n), :])
        k = both(lambda n, d: k_s[rows(n), :])
        v = both(lambda n, d: v_s[rows(n), :])
        gi = both(lambda n, d: (gf_s if d == 0 else gb_s)[rows(n), :])
        bi = both(lambda n, d: (bf_s if d == 0 else bb_s)[rows(n), :])
        diff = each(lambda g_: g_[:, 0:CHUNK] - g_.T[0:CHUNK, :], gi)
        decay = each(lambda m, df: jnp.where(m, jnp.exp(jnp.where(m, df, 0.0)), 0.0), incl, diff)
        kb = each(lambda k_, b_: k_ * b_, k, bi)
        kk = each(_mm_nt, kb, k)
        low = each(lambda m, kk_, dec: jnp.where(m, kk_ * dec, 0.0), strict, kk, decay)
        inv = [eye - jnp.where(pair_mask[1], l_, 0.0) for l_ in low]
        size = 2
        while size < CHUNK:
            coupling = [jnp.where(pair_mask[size], l_, 0.0) for l_ in low]
            left = each(lambda a_, b_: _mm_inv(a_, b_), inv, coupling)
            corr = each(lambda a_, b_: _mm_inv(a_, b_), left, inv)
            inv = each(lambda t, u_: t - u_, inv, corr)
            size *= 2
        eg = [jnp.exp(g_) for g_ in gi]
        rhs = each(lambda kb_, e_, v_, b_: jnp.concatenate([kb_ * e_, v_ * b_], axis=1), kb, eg, v, bi)
        sol = each(_mm, inv, rhs)
        qk = each(_mm_nt, q, k)
        intra = each(lambda m, qk_, dec: jnp.where(m, qk_ * dec, 0.0), incl, qk, decay)
        g_last = [g_[CHUNK - 1:CHUNK, :] if d == 0 else g_[0:1, :] for g_, (_, d) in zip(gi, chains)]
        k_tail_t = each(lambda k_, gl, g_: (k_ * jnp.exp(gl - g_)).T, k, g_last, gi)
        iwu = each(_mm, intra, sol)
        kwu = each(_mm, k_tail_t, sol)
        for idx, (n, d) in enumerate(chains):
            sk = pl.ds(pl.multiple_of(n * hd, hd), hd)
            qe_s[d, rows(n), :] = (q[idx] * eg[idx] - iwu[idx][:, 0:hd]).astype(BF16)
            o0_s[d, rows(n), :] = iwu[idx][:, hd:]
            kw_s[d, sk, :] = kwu[idx][:, 0:hd].astype(BF16)
            c_s[d, sk, :] = kwu[idx][:, hd:]
            dc_s[d, pl.ds(pl.multiple_of(n * 8, 8), 8), :] = jnp.broadcast_to(jnp.exp(g_last[idx]), (8, LANES))
        return carry

    lax.fori_loop(0, n_tot // group, stage1, 0)

    def advance(direction, n, state):
        sl = pl.ds(pl.multiple_of(n * CHUNK, CHUNK), CHUNK)
        sk = pl.ds(pl.multiple_of(n * hd, hd), hd)
        sb = state.astype(BF16)
        o_s[direction, sl, :] = (jnp.dot(qe_s[direction, sl, :], sb, preferred_element_type=F32)
                                 + o0_s[direction, sl, :])
        dc = dc_s[direction, pl.ds(pl.multiple_of(n * 8, 8), 1), :]
        return (state * dc - jnp.dot(kw_s[direction, sk, :], sb, preferred_element_type=F32)
                + c_s[direction, sk, :])

    def stage2(t, states):
        n_b = jnp.where(t < n_ctx, n_ctx - 1 - t, n_tot + n_ctx - 1 - t)
        return advance(0, t, states[0]), advance(1, n_b, states[1])

    zero = jnp.zeros((hd, hd), F32)
    lax.fori_loop(0, n_tot, stage2, (zero, zero))

    o = o_s[0] + o_s[1]
    o = o * lax.rsqrt(jnp.mean(o * o, axis=-1, keepdims=True) + RMS_EPS) * dnw_ref[...]
    oz_ref[0] = o[0:m_len, :] * _silu(zz_ref[0])
    ox_ref[0] = o[m_len:, :] * _silu(zx_ref[0])


def _delta(pz, px, gates, conv_w, par, dn_norm_w):
    b, m_len, _ = pz.shape
    l_len = px.shape[1]
    ltot = m_len + l_len
    n_tot = ltot // CHUNK
    hb = LANES

    def col(c0, length):
        return pl.BlockSpec((1, length, hb), lambda i, h: (i, 0, c0 // hb + h))

    scratch = [pltpu.VMEM((ltot, LANES), F32) for _ in range(7)]
    scratch += [pltpu.VMEM((2, ltot, LANES), F32),
                pltpu.VMEM((ltot + 2 * CHUNK, LANES), F32),
                pltpu.VMEM((2, n_tot * DN_HEAD_DIM, LANES), BF16),
                pltpu.VMEM((2, n_tot * DN_HEAD_DIM, LANES), F32),
                pltpu.VMEM((2, ltot, LANES), BF16),
                pltpu.VMEM((2, ltot, LANES), F32),
                pltpu.VMEM((2, n_tot * 8, LANES), F32)]
    return pl.pallas_call(
        functools.partial(_delta_kernel, m_len=m_len, l_len=l_len),
        grid=(b, DN_HEADS),
        in_specs=[col(COL_Q, m_len), col(COL_K, m_len), col(COL_V, m_len), col(COL_Z, m_len),
                  col(COL_Q, l_len), col(COL_K, l_len), col(COL_V, l_len), col(COL_Z, l_len),
                  pl.BlockSpec((1, 1, ltot, 4), lambda i, h: (i, h, 0, 0)),
                  pl.BlockSpec((1, 3, CONV_WIDTH, hb), lambda i, h: (h, 0, 0, 0)),
                  pl.BlockSpec((1, 8, hb), lambda i, h: (h, 0, 0)),
                  pl.BlockSpec((1, hb), lambda i, h: (0, 0))],
        out_specs=[pl.BlockSpec((1, l_len, hb), lambda i, h: (i, 0, h)),
                   pl.BlockSpec((1, m_len, hb), lambda i, h: (i, 0, h))],
        out_shape=[jax.ShapeDtypeStruct((b, l_len, DN_WIDTH), F32),
                   jax.ShapeDtypeStruct((b, m_len, DN_WIDTH), F32)],
        scratch_shapes=scratch,
        compiler_params=_params(("parallel", "parallel")),
        name="delta",
    )(pz, pz, pz, pz, px, px, px, px, gates, conv_w, par, dn_norm_w)


def _fnet_weight_kernel(c_ref, s_ref, w_ref, o_ref, *, scale):
    w = w_ref[...]
    o_ref[:, 0:FNET_WIDTH] = _mm_f32(c_ref[...], w) * scale
    o_ref[:, FNET_WIDTH:] = _mm_f32(s_ref[...], w) * scale


def _fnet_weights(w_bd, seq_len):
    n = FNET_GROUP_DIM
    ang = 2.0 * np.pi * ((np.arange(n)[:, None] * np.arange(n)[None, :]) % n) / n
    eye4 = np.eye(FNET_WIDTH // n)
    c_bd = jnp.asarray(np.kron(eye4, np.cos(ang)), F32)
    s_bd = jnp.asarray(np.kron(eye4, np.sin(ang)), F32)
    return pl.pallas_call(
        functools.partial(_fnet_weight_kernel, scale=float(1.0 / math.sqrt(seq_len * n))),
        out_shape=jax.ShapeDtypeStruct((FNET_WIDTH, 2 * FNET_WIDTH), F32),
        name="fnet_weights",
    )(c_bd, s_bd, w_bd)


def _fnet_kernel(x_ref, w_ref, c_ref, s_ref, o_ref):
    xw = _mm(x_ref[0], w_ref[...])
    o_ref[0] = _mm(c_ref[...], xw[:, 0:FNET_WIDTH]) + _mm(s_ref[...], xw[:, FNET_WIDTH:])


def _dft_mats(seq_len):
    idx = (np.arange(seq_len)[:, None] * np.arange(seq_len)[None, :]) % seq_len
    ang = 2.0 * np.pi * idx / seq_len
    return jnp.asarray(np.cos(ang), F32).astype(BF16), jnp.asarray(-np.sin(ang), F32).astype(BF16)


def _fnet(p, wcs):
    b, l, _ = p.shape
    tm = min(512, l)
    cm, sm = _dft_mats(l)
    return pl.pallas_call(
        _fnet_kernel,
        grid=(l // tm, b),
        in_specs=[pl.BlockSpec((1, l, FNET_WIDTH), lambda i, j: (j, 0, COL_FNET // FNET_WIDTH)),
                  pl.BlockSpec((FNET_WIDTH, 2 * FNET_WIDTH), lambda i, j: (0, 0)),
                  pl.BlockSpec((tm, l), lambda i, j: (i, 0)),
                  pl.BlockSpec((tm, l), lambda i, j: (i, 0))],
        out_specs=pl.BlockSpec((1, tm, FNET_WIDTH), lambda i, j: (j, i, 0)),
        out_shape=jax.ShapeDtypeStruct((b, l, FNET_WIDTH), F32),
        compiler_params=_params(("parallel", "parallel")),
        name="fnet",
    )(p, wcs, cm, sm)


def _mixout_kernel(x_ref, dn_ref, p_ref, f_ref, band_ref, icnt_ref, wp_ref, ps_ref, wo_ref,
                   mod_ref, o_ref):
    xp = p_ref[0]
    group = lax.broadcasted_iota(jnp.int32, xp.shape, 1) // POOL_GROUP_DIM
    win_sum = jnp.zeros(xp.shape, F32)
    for g in range(len(POOL_WINDOWS)):
        win_sum = win_sum + _mm(band_ref[g], jnp.where(group == g, xp, 0.0))
    y = win_sum * icnt_ref[...] - xp
    pool = _mm(y, wp_ref[...]) * ps_ref[...]
    out = (_mm(dn_ref[0], wo_ref[0:DN_WIDTH, :])
           + _mm(pool, wo_ref[DN_WIDTH:DN_WIDTH + POOL_WIDTH, :])
           + _mm(f_ref[0], wo_ref[DN_WIDTH + POOL_WIDTH:, :]))
    o_ref[0] = x_ref[0] + mod_ref[0, 2:3, :] * out


def _pool_consts(tm, seg):
    pos = np.arange(tm) % seg
    seg_id = np.arange(tm) // seg
    band = np.zeros((len(POOL_WINDOWS), tm, tm), np.float32)
    icnt = np.zeros((tm, POOL_WIDTH), np.float32)
    for g, w in enumerate(POOL_WINDOWS):
        lo = np.clip(pos - w // 2, 0, seg)
        hi = np.clip(pos + w - w // 2, 0, seg)
        j = np.arange(tm)
        inside = (seg_id[:, None] == seg_id[None, :]) & (pos[None, :] >= lo[:, None]) & (pos[None, :] < hi[:, None])
        band[g] = inside.astype(np.float32)
        icnt[:, g * POOL_GROUP_DIM:(g + 1) * POOL_GROUP_DIM] = (1.0 / (hi - lo))[:, None]
    return jnp.asarray(band).astype(BF16), jnp.asarray(icnt)


def _mixout(x, dn, p, fn, seg, wp_bd, pool_scale, w_out, mods):
    b, l, d = x.shape
    tm = min(256, l)
    band, icnt = _pool_consts(tm, min(seg, tm))
    if seg > tm:
        raise ValueError("pooling segment longer than the token tile")
    return pl.pallas_call(
        _mixout_kernel,
        grid=(b, l // tm),
        in_specs=[pl.BlockSpec((1, tm, d), lambda i, j: (i, j, 0)),
                  pl.BlockSpec((1, tm, DN_WIDTH), lambda i, j: (i, j, 0)),
                  pl.BlockSpec((1, tm, POOL_WIDTH), lambda i, j: (i, j, COL_POOL // POOL_WIDTH)),
                  pl.BlockSpec((1, tm, FNET_WIDTH), lambda i, j: (i, j, 0)),
                  pl.BlockSpec((len(POOL_WINDOWS), tm, tm), lambda i, j: (0, 0, 0)),
                  pl.BlockSpec((tm, POOL_WIDTH), lambda i, j: (0, 0)),
                  pl.BlockSpec((POOL_WIDTH, POOL_WIDTH), lambda i, j: (0, 0)),
                  pl.BlockSpec((1, POOL_WIDTH), lambda i, j: (0, 0)),
                  pl.BlockSpec((d, d), lambda i, j: (0, 0)),
                  pl.BlockSpec((1, 8, d), lambda i, j: (i, 0, 0))],
        out_specs=pl.BlockSpec((1, tm, d), lambda i, j: (i, j, 0)),
        out_shape=jax.ShapeDtypeStruct((b, l, d), F32),
        compiler_params=_params(("parallel", "parallel")),
        name="mixout",
    )(x, dn, p, fn, band, icnt, wp_bd, pool_scale, w_out, mods)


def _peer_kernel(x_ref, mod_ref, nw_ref, fw_ref, wq_ref, keys_ref, u_ref, vt_ref, o_ref,
                 ht_s, c1_s, e1_s, r2_s, e2_s, acc_s, act_s, acta_s, wga_s, wgb_s, sc_s, rc_s, re_s,
                 *, n_chunks, blocks_per_half, final_norm):
    p = pl.program_id(2)
    neg_inf = float("-inf")
    half = blocks_per_half * N_KEYS
    n_lane_groups = x_ref.shape[1] // LANES
    d_model = x_ref.shape[2]

    @pl.when(p == 0)
    def _route():
        h = _norm_mod(x_ref[0], nw_ref[...], mod_ref[0, 3:4, :], mod_ref[0, 4:5, :])
        ht_s[...] = h.T.astype(BF16)

        marker = 2.0 ** 100

        def extract(s):
            vals = []
            for r in range(PEER_TOPK):
                mx = jnp.max(s, axis=0, keepdims=True)
                vals.append(mx)
                s = jnp.where(s == mx, -marker * (r + 1), s)
            rank = jnp.where(s < -0.5 * marker, s * (-1.0 / marker) - 1.0, 99.0)
            return vals, rank

        pairs = [(a, b) for a in range(PEER_TOPK) for b in range(PEER_TOPK)
                 if (a + 1) * (b + 1) <= PEER_TOPK]
        n_pad = (-len(pairs)) % 8

        for hd in range(PEER_HEADS):
            qt = jnp.dot(wq_ref[hd * 2 * N_KEYS:(hd + 1) * 2 * N_KEYS, :], ht_s[...],
                         preferred_element_type=F32)
            sc_s[0] = _mm(keys_ref[hd, 0], qt[0:N_KEYS, :])
            sc_s[1] = _mm(keys_ref[hd, 1], qt[N_KEYS:, :])
            for tg in range(n_lane_groups):
                ln = slice(tg * LANES, (tg + 1) * LANES)
                s1 = sc_s[0, :, ln]
                s2 = sc_s[1, :, ln]
                t1, rank1 = extract(s1)
                t2, rank2 = extract(s2)
                cands = [t1[a] + t2[b] for a, b in pairs]
                cand = jnp.concatenate(cands + [jnp.full_like(cands[0], neg_inf)] * n_pad, axis=0)
                work = cand
                thr = jnp.full_like(cands[0], neg_inf)
                found = jnp.zeros_like(cands[0])
                for _ in range(PEER_TOPK):
                    mx = jnp.max(work, axis=0, keepdims=True)
                    cnt = jnp.sum(jnp.where(cand >= mx, 1.0, 0.0), axis=0, keepdims=True)
                    hit = jnp.where(cnt >= PEER_TOPK, 1.0 - found, 0.0)
                    thr = jnp.where(hit > 0.0, mx, thr)
                    found = jnp.maximum(found, hit)
                    work = jnp.where(work == mx, neg_inf, work)
                z = jnp.sum(jnp.where(cand >= thr, jnp.exp(cand - cands[0]), 0.0), axis=0, keepdims=True)
                cnt1 = jnp.zeros(s1.shape, F32)
                for a in range(PEER_TOPK):
                    partners = jnp.zeros_like(thr)
                    for idx, (pa, _) in enumerate(pairs):
                        if pa == a:
                            partners = partners + jnp.where(cands[idx] >= thr, 1.0, 0.0)
                    cnt1 = jnp.where(rank1 == float(a), partners, cnt1)
                c1_s[hd, :, ln] = cnt1
                e1_s[hd, :, ln] = jnp.exp(s1 - t1[0]) * (1.0 / z)
                r2_s[hd, :, ln] = rank2.astype(BF16)
                e2_s[hd, :, ln] = jnp.exp(s2 - t2[0]).astype(BF16)
        acc_s[...] = jnp.zeros(acc_s.shape, F32)
        act_s[...] = jnp.zeros(act_s.shape, F32)
        wga_s[...] = jnp.zeros(wga_s.shape, BF16)

    def row_bf16(row):
        tile = jnp.broadcast_to(row, (16, LANES)).astype(BF16)
        return jnp.concatenate([tile] * (N_KEYS // 16), axis=0)

    def stage_rows(first_block, slot):
        for ii in range(blocks_per_half):
            i = jnp.maximum(first_block + ii, 0)
            for hd in range(PEER_HEADS):
                k = ii * PEER_HEADS + hd
                rc_s[slot, k:k + 1, :] = c1_s[hd, pl.ds(i, 1), :]
                re_s[slot, k:k + 1, :] = e1_s[hd, pl.ds(i, 1), :]

    def gate_unit(act, slot, out_ref, tg, g0):
        ln = slice(tg * LANES, (tg + 1) * LANES)
        blocks = list(range(g0, min(g0 + 4, blocks_per_half)))
        gates = [jnp.zeros((N_KEYS, LANES), BF16) for _ in blocks]
        for hd in range(PEER_HEADS):
            r2 = r2_s[hd, :, ln]
            e2 = e2_s[hd, :, ln]
            for n, ii in enumerate(blocks):
                k = ii * PEER_HEADS + hd
                c1row = row_bf16(rc_s[slot, k:k + 1, ln])
                e1row = row_bf16(re_s[slot, k:k + 1, ln])
                gates[n] = gates[n] + jnp.minimum(jnp.maximum(c1row - r2, 0.0), e1row) * e2
        for n, ii in enumerate(blocks):
            a = act[ii * N_KEYS:(ii + 1) * N_KEYS, ln]
            gelu = 0.5 * a * (1.0 + lax.erf(a * (2.0 ** -0.5)))
            out_ref[ii * N_KEYS:(ii + 1) * N_KEYS, ln] = gates[n] * gelu.astype(BF16)

    def phase(matmul_piece, act, slot, out_ref, lane_groups):
        units = [(tg, g0) for tg in lane_groups for g0 in range(0, blocks_per_half, 4)]
        matmul_piece(0, 1)
        for unit in units:
            gate_unit(act, slot, out_ref, *unit)

    def value_piece(lo, w_ref):
        def piece(idx, n_pieces):
            rows = d_model // n_pieces
            r0 = idx * rows
            acc_s[r0:r0 + rows, :] += jnp.dot(vt_ref[r0:r0 + rows, lo:lo + half], w_ref[...],
                                              preferred_element_type=F32)
        return piece

    def act_piece(lo, dst_ref):
        def piece(idx, n_pieces):
            rows = half // n_pieces
            r0 = idx * rows
            dst_ref[r0:r0 + rows, :] = jnp.dot(u_ref[lo + r0:lo + r0 + rows, :], ht_s[...],
                                               preferred_element_type=F32)
        return piece

    prev_b = (2 * p - 1) * blocks_per_half

    first = tuple(range(n_lane_groups // 2))
    second = tuple(range(n_lane_groups // 2, n_lane_groups))

    @pl.when(p < n_chunks)
    def _stage():
        stage_rows(prev_b, 0)
        stage_rows(2 * p * blocks_per_half, 1)

    @pl.when(p + 1 <= n_chunks)
    def _phase1():
        phase(value_piece(0, wga_s), act_s, 0, wgb_s, first)

    @pl.when(p + 2 <= n_chunks + 1)
    def _phase2():
        phase(act_piece(0, acta_s), act_s, 0, wgb_s, second)

    @pl.when(p + 3 <= n_chunks + 2)
    def _phase3():
        phase(value_piece(half, wgb_s), acta_s, 1, wga_s, first)

    @pl.when(p + 4 <= n_chunks + 3)
    def _phase4():
        phase(act_piece(half, act_s), acta_s, 1, wga_s, second)

    @pl.when(p == n_chunks)
    def _drain1():
        stage_rows(prev_b, 0)
        phase(value_piece(0, wga_s), act_s, 0, wgb_s, first + second)

    @pl.when(p >= n_chunks)
    def _drain2():
        out = acc_s[...] + jnp.dot(vt_ref[:, half:], wgb_s[...], preferred_element_type=F32)
        y = x_ref[0] + mod_ref[0, 5:6, :] * out.T
        if final_norm:
            y = y * lax.rsqrt(jnp.mean(y * y, axis=-1, keepdims=True) + RMS_EPS) * fw_ref[...]
        o_ref[0] = y


def _peer(x, mods, nw, wq_t, keys, u, v_t, final_w, final_norm, tm=512, chunk=2048):
    b, l, d = x.shape
    tm = min(tm, l)
    n_chunks = N_EXPERTS // chunk
    half = chunk // 2
    kern = functools.partial(_peer_kernel, n_chunks=n_chunks, blocks_per_half=half // N_KEYS,
                             final_norm=final_norm)
    head_shape = (PEER_HEADS, N_KEYS, tm)
    once = pl.Buffered(1)
    return pl.pallas_call(
        kern,
        grid=(b, l // tm, n_chunks + 1),
        in_specs=[pl.BlockSpec((1, tm, d), lambda i, t, j: (i, t, 0)),
                  pl.BlockSpec((1, 8, d), lambda i, t, j: (i, 0, 0)),
                  pl.BlockSpec((1, d), lambda i, t, j: (0, 0)),
                  pl.BlockSpec((1, d), lambda i, t, j: (0, 0)),
                  pl.BlockSpec(wq_t.shape, lambda i, t, j: (0, 0), pipeline_mode=once),
                  pl.BlockSpec(keys.shape, lambda i, t, j: (0, 0, 0, 0), pipeline_mode=once),
                  pl.BlockSpec((chunk, d), lambda i, t, j: (jnp.minimum(j, n_chunks - 1), 0)),
                  pl.BlockSpec((d, chunk), lambda i, t, j: (0, jnp.maximum(j - 1, 0)))],
        out_specs=pl.BlockSpec((1, tm, d), lambda i, t, j: (i, t, 0)),
        out_shape=jax.ShapeDtypeStruct((b, l, d), F32),
        scratch_shapes=[pltpu.VMEM((d, tm), BF16),
                        pltpu.VMEM(head_shape, F32), pltpu.VMEM(head_shape, F32),
                        pltpu.VMEM(head_shape, BF16), pltpu.VMEM(head_shape, BF16),
                        pltpu.VMEM((d, tm), F32),
                        pltpu.VMEM((half, tm), F32), pltpu.VMEM((half, tm), F32),
                        pltpu.VMEM((half, tm), BF16), pltpu.VMEM((half, tm), BF16),
                        pltpu.VMEM((2, N_KEYS, tm), F32),
                        pltpu.VMEM((2, half // N_KEYS * PEER_HEADS, tm), F32),
                        pltpu.VMEM((2, half // N_KEYS * PEER_HEADS, tm), F32)],
        compiler_params=_params(("parallel", "parallel", "arbitrary")),
        name="peer",
    )(x, mods, nw, final_w, wq_t, keys, u, v_t)


def _block_diag(w):
    g, n, _ = w.shape
    out = jnp.zeros((g * n, g * n), w.dtype)
    for i in range(g):
        out = out.at[i * n:(i + 1) * n, i * n:(i + 1) * n].set(w[i])
    return out


def _reorder_w_in(w):
    d = w.shape[0]
    return jnp.concatenate([w[:, :REF_QKV], w[:, REF_ALPHA:REF_Z], w[:, REF_Z:REF_POOL],
                            w[:, REF_POOL:REF_FNET], w[:, REF_QKV:REF_ALPHA],
                            jnp.zeros((d, IN_PAD - REF_FNET), w.dtype)], axis=1).astype(BF16)


def _gate_columns(gz, gx):
    ba = jnp.concatenate([gz[:, :, 0:16], gx[:, :, 0:16]], axis=1)
    b, ltot, _ = ba.shape
    return ba.reshape(b, ltot, 4, DN_HEADS).transpose(0, 3, 1, 2)


def _mod_rows(mod, rows):
    d = mod.shape[1] // N_MOD
    m = mod.reshape(16, N_MOD, d)[rows]
    return jnp.concatenate([m, jnp.zeros((m.shape[0], 8 - N_MOD, d), m.dtype)], axis=1)


def kernel(x, c, ctx, c_ctx, w_mod, b_mod, norm1_w, norm2_w, w_in, conv_w, a_log, dt_bias,
           dn_norm_w, w_pool, pool_scale, w_fnet, w_out, w_query, sub_keys, expert_u, expert_v,
           final_norm_w):
    b, l, d = x.shape
    m = ctx.shape[1]
    depth = w_mod.shape[0]
    rows = l // GRID_W

    cs = jnp.concatenate([c, c_ctx[None, :], jnp.zeros((16 - b - 1, d), F32)], axis=0)
    mod_all = _modulation(cs, w_mod, b_mod)

    z = ctx
    for i in range(depth):
        update_ctx = i < depth - 1
        mods_x = _mod_rows(mod_all[i], jnp.arange(b))
        mods_z = _mod_rows(mod_all[i], jnp.full((b,), b))
        w_in_r = _reorder_w_in(w_in[i])
        nw1 = norm1_w[i][None, :]
        nw2 = norm2_w[i][None, :]

        px, gx = _inproj(x, mods_x, nw1, w_in_r)
        pz, gz = _inproj(z, mods_z, nw1, w_in_r)

        cw = conv_w[i].reshape(CONV_WIDTH, 3, DN_HEADS, DN_HEAD_DIM).transpose(2, 1, 0, 3)
        par = jnp.concatenate([a_log[i], dt_bias[i]], axis=0)
        par = jnp.broadcast_to(par.T[:, :, None], (DN_HEADS, 4, LANES))
        par = jnp.concatenate([par, jnp.zeros((DN_HEADS, 4, LANES), F32)], axis=1)
        dn_x, dn_z = _delta(pz, px, _gate_columns(gz, gx), cw, par, dn_norm_w[i][None, :])

        wp_bd = _block_diag(w_pool[i]).astype(BF16)
        wf_bd = _block_diag(w_fnet[i])
        w_out_b = w_out[i].astype(BF16)
        ps = pool_scale[i][None, :]

        fn_x = _fnet(px, _fnet_weights(wf_bd, l).astype(BF16))
        x = _mixout(x, dn_x, px, fn_x, l // rows, wp_bd, ps, w_out_b, mods_x)
        if update_ctx:
            fn_z = _fnet(pz, _fnet_weights(wf_bd, m).astype(BF16))
            z = _mixout(z, dn_z, pz, fn_z, m, wp_bd, ps, w_out_b, mods_z)

        wq_t = w_query[i].T.astype(BF16)
        keys = sub_keys[i].astype(BF16)
        u_b = expert_u[i].astype(BF16)
        v_t = expert_v[i].T.astype(BF16)
        fw = final_norm_w[None, :]
        x = _peer(x, mods_x, nw2, wq_t, keys, u_b, v_t, fw, not update_ctx)
        if update_ctx:
            z = _peer(z, mods_z, nw2, wq_t, keys, u_b, v_t, fw, False)

    return x
```

```python
import functools
import math

import numpy as np
import jax
import jax.numpy as jnp
from jax import lax
from jax.experimental import pallas as pl
from jax.experimental.pallas import tpu as pltpu

F32 = jnp.float32
BF16 = jnp.bfloat16
HIGHEST = lax.Precision.HIGHEST

D_MODEL = 1024
DN_HEADS = 4
DN_HEAD_DIM = 128
DN_WIDTH = DN_HEADS * DN_HEAD_DIM
POOL_WINDOWS = (2, 4, 8, 16)
POOL_WIDTH = 256
POOL_GROUP_DIM = 64
FNET_WIDTH = 256
FNET_GROUP_DIM = 64
CONV_WIDTH = 4
CHUNK = 64
GRID_W = 64
PEER_HEADS = 8
N_KEYS = 128
N_EXPERTS = N_KEYS * N_KEYS
PEER_TOPK = 16
N_MOD = 6
RMS_EPS = 1e-6
L2_EPS = 1e-6

COL_Q, COL_K, COL_V, COL_Z = 0, 512, 1024, 1536
COL_POOL, COL_FNET, COL_BA = 2048, 2304, 2560
IN_PAD = 2688
REF_QKV, REF_BETA, REF_ALPHA, REF_Z, REF_POOL, REF_FNET = 1536, 1544, 1552, 2064, 2320, 2576

LANES = 128
VMEM_LIMIT = 56 * 1024 * 1024


def _params(sem, flags=None):
    return pltpu.CompilerParams(dimension_semantics=sem, vmem_limit_bytes=VMEM_LIMIT, flags=flags)


def _mm(a, b):
    return jnp.dot(a.astype(BF16), b.astype(BF16), preferred_element_type=F32)


def _mm_nt(a, b):
    return lax.dot_general(a.astype(BF16), b.astype(BF16), (((1,), (1,)), ((), ())),
                           preferred_element_type=F32)


def _mm_inv(a, b):
    return _mm(a, b)


def _mm_f32(a, b):
    return jnp.dot(a, b, precision=HIGHEST, preferred_element_type=F32)


def _silu(x):
    return x * jax.nn.sigmoid(x)


def _mod_kernel(c_ref, w_ref, b_ref, o_ref):
    o_ref[0] = _mm_f32(_silu(c_ref[...]), w_ref[0]) + b_ref[0]


def _modulation(cs, w_mod, b_mod):
    depth, d, n = w_mod.shape
    tn = 1536
    return pl.pallas_call(
        _mod_kernel,
        grid=(depth, n // tn),
        in_specs=[pl.BlockSpec((16, d), lambda i, j: (0, 0)),
                  pl.BlockSpec((1, d, tn), lambda i, j: (i, 0, j)),
                  pl.BlockSpec((1, 1, tn), lambda i, j: (i, 0, j))],
        out_specs=pl.BlockSpec((1, 16, tn), lambda i, j: (i, 0, j)),
        out_shape=jax.ShapeDtypeStruct((depth, 16, n), F32),
        compiler_params=_params(("parallel", "parallel")),
        name="modulation",
    )(cs, w_mod, b_mod.reshape(depth, 1, n))


def _norm_mod(x, nw, shift, scale):
    y = x * lax.rsqrt(jnp.mean(x * x, axis=-1, keepdims=True) + RMS_EPS) * nw
    return y * (1.0 + scale) + shift


def _inproj_kernel(x_ref, mod_ref, nw_ref, w_ref, o_ref, g_ref):
    h = _norm_mod(x_ref[0], nw_ref[...], mod_ref[0, 0:1, :], mod_ref[0, 1:2, :])
    p = _mm(h, w_ref[...])
    o_ref[0] = p
    g_ref[0] = p[:, COL_BA:]


def _inproj(x, mods, nw, w):
    b, l, d = x.shape
    n = w.shape[1]
    tm = min(512, l)
    return pl.pallas_call(
        _inproj_kernel,
        grid=(b, l // tm),
        in_specs=[pl.BlockSpec((1, tm, d), lambda i, j: (i, j, 0)),
                  pl.BlockSpec((1, 8, d), lambda i, j: (i, 0, 0)),
                  pl.BlockSpec((1, d), lambda i, j: (0, 0)),
                  pl.BlockSpec((d, n), lambda i, j: (0, 0))],
        out_specs=[pl.BlockSpec((1, tm, n), lambda i, j: (i, j, 0)),
                   pl.BlockSpec((1, tm, n - COL_BA), lambda i, j: (i, j, 0))],
        out_shape=[jax.ShapeDtypeStruct((b, l, n), F32),
                   jax.ShapeDtypeStruct((b, l, n - COL_BA), F32)],
        compiler_params=_params(("parallel", "parallel")),
        name="inproj",
    )(x, mods, nw, w)


def _delta_kernel(qz_ref, kz_ref, vz_ref, zz_ref, qx_ref, kx_ref, vx_ref, zx_ref,
                  gate_ref, cw_ref, par_ref, dnw_ref, ox_ref, oz_ref,
                  q_s, k_s, v_s, gf_s, gb_s, bf_s, bb_s, o_s, pad_s, kw_s, c_s, qe_s, o0_s, dc_s,
                  *, m_len, l_len):
    ltot = m_len + l_len
    n_ctx = m_len // CHUNK
    n_tot = ltot // CHUNK
    off = CHUNK

    def conv_silu(src_ref, w, length):
        pad_s[off - 8:off, :] = jnp.zeros((8, LANES), F32)
        pad_s[off:off + length, :] = src_ref[0]
        pad_s[off + length:off + length + 8, :] = jnp.zeros((8, LANES), F32)
        y = (w[0:1, :] * pad_s[off - 2:off - 2 + length, :]
             + w[1:2, :] * pad_s[off - 1:off - 1 + length, :]
             + w[2:3, :] * pad_s[off:off + length, :]
             + w[3:4, :] * pad_s[off + 1:off + 1 + length, :])
        return _silu(y)

    def l2norm(t):
        return t * lax.rsqrt(jnp.sum(t * t, axis=-1, keepdims=True) + L2_EPS)

    for (src_q, src_k, src_v, start, length) in ((qz_ref, kz_ref, vz_ref, 0, m_len),
                                                 (qx_ref, kx_ref, vx_ref, m_len, l_len)):
        q_s[start:start + length, :] = l2norm(conv_silu(src_q, cw_ref[0, 0], length)) * (DN_HEAD_DIM ** -0.5)
        k_s[start:start + length, :] = l2norm(conv_silu(src_k, cw_ref[0, 1], length))
        v_s[start:start + length, :] = conv_silu(src_v, cw_ref[0, 2], length)

    gt = gate_ref[0, 0]
    par = par_ref[0]
    shape = (ltot, LANES)
    def softplus(t):
        return jnp.maximum(t, 0.0) + jnp.log1p(jnp.exp(-jnp.abs(t)))

    beta = jax.nn.sigmoid(gt)
    a_log = jnp.concatenate([par[0:1, 0:2], par[0:1, 0:1], par[1:2, 0:1]], axis=1)
    dt_bias = jnp.concatenate([par[0:1, 0:2], par[2:3, 0:1], par[3:4, 0:1]], axis=1)
    log_decay = -jnp.exp(a_log) * softplus(gt + dt_bias)
    bf_s[...] = jnp.broadcast_to(beta[:, 0:1], shape)
    bb_s[...] = jnp.broadcast_to(beta[:, 1:2], shape)
    gf_s[...] = jnp.broadcast_to(log_decay[:, 2:3], shape)
    gb_s[...] = jnp.broadcast_to(log_decay[:, 3:4], shape)

    pos = lax.broadcasted_iota(jnp.int32, shape, 0) & (CHUNK - 1)
    pad_s[0:off, :] = jnp.zeros((off, LANES), F32)
    pad_s[off + ltot:off + ltot + off, :] = jnp.zeros((off, LANES), F32)
    step = 1
    while step < CHUNK:
        pad_s[off:off + ltot, :] = gf_s[...]
        gf_s[...] = gf_s[...] + jnp.where(pos >= step, pad_s[off - step:off - step + ltot, :], 0.0)
        pad_s[off:off + ltot, :] = gb_s[...]
        gb_s[...] = gb_s[...] + jnp.where(pos < CHUNK - step, pad_s[off + step:off + step + ltot, :], 0.0)
        step *= 2

    ri = lax.broadcasted_iota(jnp.int32, (CHUNK, CHUNK), 0)
    ci = lax.broadcasted_iota(jnp.int32, (CHUNK, CHUNK), 1)
    eye = (ri == ci).astype(F32)
    hd = DN_HEAD_DIM
    pair_mask = {1 << b: jnp.logical_and((ri >> (b + 1)) == (ci >> (b + 1)), (ri >> b) != (ci >> b))
                 for b in range(6)}

    group = max(g for g in range(1, 10) if n_tot % g == 0)

    def stage1(g, carry):
        chains = [(g * group + c, d) for c in range(group) for d in (0, 1)]
        both = lambda fn: [fn(n, d) for (n, d) in chains]
        each = lambda fn, *cols: [fn(*vals) for vals in zip(*cols)]
        rows = lambda n: pl.ds(pl.multiple_of(n * CHUNK, CHUNK), CHUNK)
        incl = both(lambda n, d: (ri >= ci) if d == 0 else (ri <= ci))
        strict = both(lambda n, d: (ri > ci) if d == 0 else (ri < ci))
        q = both(lambda n, d: q_s[rows(n), :])
        k = both(lambda n, d: k_s[rows(n), :])
        v = both(lambda n, d: v_s[rows(n), :])
        gi = both(lambda n, d: (gf_s if d == 0 else gb_s)[rows(n), :])
        bi = both(lambda n, d: (bf_s if d == 0 else bb_s)[rows(n), :])
        diff = each(lambda g_: g_[:, 0:CHUNK] - g_.T[0:CHUNK, :], gi)
        decay = each(lambda m, df: jnp.where(m, jnp.exp(jnp.where(m, df, 0.0)), 0.0), incl, diff)
        kb = each(lambda k_, b_: k_ * b_, k, bi)
        kk = each(_mm_nt, kb, k)
        low = each(lambda m, kk_, dec: jnp.where(m, kk_ * dec, 0.0), strict, kk, decay)
        inv = [eye - jnp.where(pair_mask[1], l_, 0.0) for l_ in low]
        size = 2
        while size < CHUNK:
            coupling = [jnp.where(pair_mask[size], l_, 0.0) for l_ in low]
            left = each(lambda a_, b_: _mm_inv(a_, b_), inv, coupling)
            corr = each(lambda a_, b_: _mm_inv(a_, b_), left, inv)
            inv = each(lambda t, u_: t - u_, inv, corr)
            size *= 2
        eg = [jnp.exp(g_) for g_ in gi]
        rhs = each(lambda kb_, e_, v_, b_: jnp.concatenate([kb_ * e_, v_ * b_], axis=1), kb, eg, v, bi)
        sol = each(_mm, inv, rhs)
        qk = each(_mm_nt, q, k)
        intra = each(lambda m, qk_, dec: jnp.where(m, qk_ * dec, 0.0), incl, qk, decay)
        g_last = [g_[CHUNK - 1:CHUNK, :] if d == 0 else g_[0:1, :] for g_, (_, d) in zip(gi, chains)]
        k_tail_t = each(lambda k_, gl, g_: (k_ * jnp.exp(gl - g_)).T, k, g_last, gi)
        iwu = each(_mm, intra, sol)
        kwu = each(_mm, k_tail_t, sol)
        for idx, (n, d) in enumerate(chains):
            sk = pl.ds(pl.multiple_of(n * hd, hd), hd)
            qe_s[d, rows(n), :] = (q[idx] * eg[idx] - iwu[idx][:, 0:hd]).astype(BF16)
            o0_s[d, rows(n), :] = iwu[idx][:, hd:]
            kw_s[d, sk, :] = kwu[idx][:, 0:hd].astype(BF16)
            c_s[d, sk, :] = kwu[idx][:, hd:]
            dc_s[d, pl.ds(pl.multiple_of(n * 8, 8), 8), :] = jnp.broadcast_to(jnp.exp(g_last[idx]), (8, LANES))
        return carry

    lax.fori_loop(0, n_tot // group, stage1, 0)

    def advance(direction, n, state):
        sl = pl.ds(pl.multiple_of(n * CHUNK, CHUNK), CHUNK)
        sk = pl.ds(pl.multiple_of(n * hd, hd), hd)
        sb = state.astype(BF16)
        o_s[direction, sl, :] = (jnp.dot(qe_s[direction, sl, :], sb, preferred_element_type=F32)
                                 + o0_s[direction, sl, :])
        dc = dc_s[direction, pl.ds(pl.multiple_of(n * 8, 8), 1), :]
        return (state * dc - jnp.dot(kw_s[direction, sk, :], sb, preferred_element_type=F32)
                + c_s[direction, sk, :])

    def stage2(t, states):
        n_b = jnp.where(t < n_ctx, n_ctx - 1 - t, n_tot + n_ctx - 1 - t)
        return advance(0, t, states[0]), advance(1, n_b, states[1])

    zero = jnp.zeros((hd, hd), F32)
    lax.fori_loop(0, n_tot, stage2, (zero, zero))

    o = o_s[0] + o_s[1]
    o = o * lax.rsqrt(jnp.mean(o * o, axis=-1, keepdims=True) + RMS_EPS) * dnw_ref[...]
    oz_ref[0] = o[0:m_len, :] * _silu(zz_ref[0])
    ox_ref[0] = o[m_len:, :] * _silu(zx_ref[0])


def _delta(pz, px, gates, conv_w, par, dn_norm_w):
    b, m_len, _ = pz.shape
    l_len = px.shape[1]
    ltot = m_len + l_len
    n_tot = ltot // CHUNK
    hb = LANES

    def col(c0, length):
        return pl.BlockSpec((1, length, hb), lambda i, h: (i, 0, c0 // hb + h))

    scratch = [pltpu.VMEM((ltot, LANES), F32) for _ in range(7)]
    scratch += [pltpu.VMEM((2, ltot, LANES), F32),
                pltpu.VMEM((ltot + 2 * CHUNK, LANES), F32),
                pltpu.VMEM((2, n_tot * DN_HEAD_DIM, LANES), BF16),
                pltpu.VMEM((2, n_tot * DN_HEAD_DIM, LANES), F32),
                pltpu.VMEM((2, ltot, LANES), BF16),
                pltpu.VMEM((2, ltot, LANES), F32),
                pltpu.VMEM((2, n_tot * 8, LANES), F32)]
    return pl.pallas_call(
        functools.partial(_delta_kernel, m_len=m_len, l_len=l_len),
        grid=(b, DN_HEADS),
        in_specs=[col(COL_Q, m_len), col(COL_K, m_len), col(COL_V, m_len), col(COL_Z, m_len),
                  col(COL_Q, l_len), col(COL_K, l_len), col(COL_V, l_len), col(COL_Z, l_len),
                  pl.BlockSpec((1, 1, ltot, 4), lambda i, h: (i, h, 0, 0)),
                  pl.BlockSpec((1, 3, CONV_WIDTH, hb), lambda i, h: (h, 0, 0, 0)),
                  pl.BlockSpec((1, 8, hb), lambda i, h: (h, 0, 0)),
                  pl.BlockSpec((1, hb), lambda i, h: (0, 0))],
        out_specs=[pl.BlockSpec((1, l_len, hb), lambda i, h: (i, 0, h)),
                   pl.BlockSpec((1, m_len, hb), lambda i, h: (i, 0, h))],
        out_shape=[jax.ShapeDtypeStruct((b, l_len, DN_WIDTH), F32),
                   jax.ShapeDtypeStruct((b, m_len, DN_WIDTH), F32)],
        scratch_shapes=scratch,
        compiler_params=_params(("parallel", "parallel")),
        name="delta",
    )(pz, pz, pz, pz, px, px, px, px, gates, conv_w, par, dn_norm_w)


def _fnet_weight_kernel(c_ref, s_ref, w_ref, o_ref, *, scale):
    w = w_ref[...]
    o_ref[:, 0:FNET_WIDTH] = _mm_f32(c_ref[...], w) * scale
    o_ref[:, FNET_WIDTH:] = _mm_f32(s_ref[...], w) * scale


def _fnet_weights(w_bd, seq_len):
    n = FNET_GROUP_DIM
    ang = 2.0 * np.pi * ((np.arange(n)[:, None] * np.arange(n)[None, :]) % n) / n
    eye4 = np.eye(FNET_WIDTH // n)
    c_bd = jnp.asarray(np.kron(eye4, np.cos(ang)), F32)
    s_bd = jnp.asarray(np.kron(eye4, np.sin(ang)), F32)
    return pl.pallas_call(
        functools.partial(_fnet_weight_kernel, scale=float(1.0 / math.sqrt(seq_len * n))),
        out_shape=jax.ShapeDtypeStruct((FNET_WIDTH, 2 * FNET_WIDTH), F32),
        name="fnet_weights",
    )(c_bd, s_bd, w_bd)


def _fnet_kernel(x_ref, w_ref, c_ref, s_ref, o_ref):
    xw = _mm(x_ref[0], w_ref[...])
    o_ref[0] = _mm(c_ref[...], xw[:, 0:FNET_WIDTH]) + _mm(s_ref[...], xw[:, FNET_WIDTH:])


def _dft_mats(seq_len):
    idx = (np.arange(seq_len)[:, None] * np.arange(seq_len)[None, :]) % seq_len
    ang = 2.0 * np.pi * idx / seq_len
    return jnp.asarray(np.cos(ang), F32).astype(BF16), jnp.asarray(-np.sin(ang), F32).astype(BF16)


def _fnet(p, wcs):
    b, l, _ = p.shape
    tm = min(512, l)
    cm, sm = _dft_mats(l)
    return pl.pallas_call(
        _fnet_kernel,
        grid=(l // tm, b),
        in_specs=[pl.BlockSpec((1, l, FNET_WIDTH), lambda i, j: (j, 0, COL_FNET // FNET_WIDTH)),
                  pl.BlockSpec((FNET_WIDTH, 2 * FNET_WIDTH), lambda i, j: (0, 0)),
                  pl.BlockSpec((tm, l), lambda i, j: (i, 0)),
                  pl.BlockSpec((tm, l), lambda i, j: (i, 0))],
        out_specs=pl.BlockSpec((1, tm, FNET_WIDTH), lambda i, j: (j, i, 0)),
        out_shape=jax.ShapeDtypeStruct((b, l, FNET_WIDTH), F32),
        compiler_params=_params(("parallel", "parallel")),
        name="fnet",
    )(p, wcs, cm, sm)


def _mixout_kernel(x_ref, dn_ref, p_ref, f_ref, band_ref, icnt_ref, wp_ref, ps_ref, wo_ref,
                   mod_ref, o_ref):
    xp = p_ref[0]
    group = lax.broadcasted_iota(jnp.int32, xp.shape, 1) // POOL_GROUP_DIM
    win_sum = jnp.zeros(xp.shape, F32)
    for g in range(len(POOL_WINDOWS)):
        win_sum = win_sum + _mm(band_ref[g], jnp.where(group == g, xp, 0.0))
    y = win_sum * icnt_ref[...] - xp
    pool = _mm(y, wp_ref[...]) * ps_ref[...]
    out = (_mm(dn_ref[0], wo_ref[0:DN_WIDTH, :])
           + _mm(pool, wo_ref[DN_WIDTH:DN_WIDTH + POOL_WIDTH, :])
           + _mm(f_ref[0], wo_ref[DN_WIDTH + POOL_WIDTH:, :]))
    o_ref[0] = x_ref[0] + mod_ref[0, 2:3, :] * out


def _pool_consts(tm, seg):
    pos = np.arange(tm) % seg
    seg_id = np.arange(tm) // seg
    band = np.zeros((len(POOL_WINDOWS), tm, tm), np.float32)
    icnt = np.zeros((tm, POOL_WIDTH), np.float32)
    for g, w in enumerate(POOL_WINDOWS):
        lo = np.clip(pos - w // 2, 0, seg)
        hi = np.clip(pos + w - w // 2, 0, seg)
        j = np.arange(tm)
        inside = (seg_id[:, None] == seg_id[None, :]) & (pos[None, :] >= lo[:, None]) & (pos[None, :] < hi[:, None])
        band[g] = inside.astype(np.float32)
        icnt[:, g * POOL_GROUP_DIM:(g + 1) * POOL_GROUP_DIM] = (1.0 / (hi - lo))[:, None]
    return jnp.asarray(band).astype(BF16), jnp.asarray(icnt)


def _mixout(x, dn, p, fn, seg, wp_bd, pool_scale, w_out, mods):
    b, l, d = x.shape
    tm = min(256, l)
    band, icnt = _pool_consts(tm, min(seg, tm))
    if seg > tm:
        raise ValueError("pooling segment longer than the token tile")
    return pl.pallas_call(
        _mixout_kernel,
        grid=(b, l // tm),
        in_specs=[pl.BlockSpec((1, tm, d), lambda i, j: (i, j, 0)),
                  pl.BlockSpec((1, tm, DN_WIDTH), lambda i, j: (i, j, 0)),
                  pl.BlockSpec((1, tm, POOL_WIDTH), lambda i, j: (i, j, COL_POOL // POOL_WIDTH)),
                  pl.BlockSpec((1, tm, FNET_WIDTH), lambda i, j: (i, j, 0)),
                  pl.BlockSpec((len(POOL_WINDOWS), tm, tm), lambda i, j: (0, 0, 0)),
                  pl.BlockSpec((tm, POOL_WIDTH), lambda i, j: (0, 0)),
                  pl.BlockSpec((POOL_WIDTH, POOL_WIDTH), lambda i, j: (0, 0)),
                  pl.BlockSpec((1, POOL_WIDTH), lambda i, j: (0, 0)),
                  pl.BlockSpec((d, d), lambda i, j: (0, 0)),
                  pl.BlockSpec((1, 8, d), lambda i, j: (i, 0, 0))],
        out_specs=pl.BlockSpec((1, tm, d), lambda i, j: (i, j, 0)),
        out_shape=jax.ShapeDtypeStruct((b, l, d), F32),
        compiler_params=_params(("parallel", "parallel")),
        name="mixout",
    )(x, dn, p, fn, band, icnt, wp_bd, pool_scale, w_out, mods)


def _peer_kernel(x_ref, mod_ref, nw_ref, fw_ref, wq_ref, keys_ref, u_ref, vt_ref, o_ref,
                 ht_s, c1_s, e1_s, r2_s, e2_s, acc_s, act_s, acta_s, wga_s, wgb_s, sc_s, rc_s, re_s,
                 *, n_chunks, blocks_per_half, final_norm):
    p = pl.program_id(2)
    neg_inf = float("-inf")
    half = blocks_per_half * N_KEYS
    n_lane_groups = x_ref.shape[1] // LANES
    d_model = x_ref.shape[2]

    @pl.when(p == 0)
    def _route():
        h = _norm_mod(x_ref[0], nw_ref[...], mod_ref[0, 3:4, :], mod_ref[0, 4:5, :])
        ht_s[...] = h.T.astype(BF16)

        marker = 2.0 ** 100

        def extract(s):
            vals = []
            for r in range(PEER_TOPK):
                mx = jnp.max(s, axis=0, keepdims=True)
                vals.append(mx)
                s = jnp.where(s == mx, -marker * (r + 1), s)
            rank = jnp.where(s < -0.5 * marker, s * (-1.0 / marker) - 1.0, 99.0)
            return vals, rank

        pairs = [(a, b) for a in range(PEER_TOPK) for b in range(PEER_TOPK)
                 if (a + 1) * (b + 1) <= PEER_TOPK]
        n_pad = (-len(pairs)) % 8

        for hd in range(PEER_HEADS):
            qt = jnp.dot(wq_ref[hd * 2 * N_KEYS:(hd + 1) * 2 * N_KEYS, :], ht_s[...],
                         preferred_element_type=F32)
            sc_s[0] = _mm(keys_ref[hd, 0], qt[0:N_KEYS, :])
            sc_s[1] = _mm(keys_ref[hd, 1], qt[N_KEYS:, :])
            for tg in range(n_lane_groups):
                ln = slice(tg * LANES, (tg + 1) * LANES)
                s1 = sc_s[0, :, ln]
                s2 = sc_s[1, :, ln]
                t1, rank1 = extract(s1)
                t2, rank2 = extract(s2)
                cands = [t1[a] + t2[b] for a, b in pairs]
                cand = jnp.concatenate(cands + [jnp.full_like(cands[0], neg_inf)] * n_pad, axis=0)
                work = cand
                thr = jnp.full_like(cands[0], neg_inf)
                found = jnp.zeros_like(cands[0])
                for _ in range(PEER_TOPK):
                    mx = jnp.max(work, axis=0, keepdims=True)
                    cnt = jnp.sum(jnp.where(cand >= mx, 1.0, 0.0), axis=0, keepdims=True)
                    hit = jnp.where(cnt >= PEER_TOPK, 1.0 - found, 0.0)
                    thr = jnp.where(hit > 0.0, mx, thr)
                    found = jnp.maximum(found, hit)
                    work = jnp.where(work == mx, neg_inf, work)
                z = jnp.sum(jnp.where(cand >= thr, jnp.exp(cand - cands[0]), 0.0), axis=0, keepdims=True)
                cnt1 = jnp.zeros(s1.shape, F32)
                for a in range(PEER_TOPK):
                    partners = jnp.zeros_like(thr)
                    for idx, (pa, _) in enumerate(pairs):
                        if pa == a:
                            partners = partners + jnp.where(cands[idx] >= thr, 1.0, 0.0)
                    cnt1 = jnp.where(rank1 == float(a), partners, cnt1)
                c1_s[hd, :, ln] = cnt1
                e1_s[hd, :, ln] = jnp.exp(s1 - t1[0]) * (1.0 / z)
                r2_s[hd, :, ln] = rank2.astype(BF16)
                e2_s[hd, :, ln] = jnp.exp(s2 - t2[0]).astype(BF16)
        acc_s[...] = jnp.zeros(acc_s.shape, F32)
        act_s[...] = jnp.zeros(act_s.shape, F32)
        wga_s[...] = jnp.zeros(wga_s.shape, BF16)

    def row_bf16(row):
        tile = jnp.broadcast_to(row, (16, LANES)).astype(BF16)
        return jnp.concatenate([tile] * (N_KEYS // 16), axis=0)

    def stage_rows(first_block, slot):
        for ii in range(blocks_per_half):
            i = jnp.maximum(first_block + ii, 0)
            for hd in range(PEER_HEADS):
                k = ii * PEER_HEADS + hd
                rc_s[slot, k:k + 1, :] = c1_s[hd, pl.ds(i, 1), :]
                re_s[slot, k:k + 1, :] = e1_s[hd, pl.ds(i, 1), :]

    def gate_unit(act, slot, out_ref, tg, g0):
        ln = slice(tg * LANES, (tg + 1) * LANES)
        blocks = list(range(g0, min(g0 + 4, blocks_per_half)))
        gates = [jnp.zeros((N_KEYS, LANES), BF16) for _ in blocks]
        for hd in range(PEER_HEADS):
            r2 = r2_s[hd, :, ln]
            e2 = e2_s[hd, :, ln]
            for n, ii in enumerate(blocks):
                k = ii * PEER_HEADS + hd
                c1row = row_bf16(rc_s[slot, k:k + 1, ln])
                e1row = row_bf16(re_s[slot, k:k + 1, ln])
                gates[n] = gates[n] + jnp.minimum(jnp.maximum(c1row - r2, 0.0), e1row) * e2
        for n, ii in enumerate(blocks):
            a = act[ii * N_KEYS:(ii + 1) * N_KEYS, ln]
            gelu = 0.5 * a * (1.0 + lax.erf(a * (2.0 ** -0.5)))
            out_ref[ii * N_KEYS:(ii + 1) * N_KEYS, ln] = gates[n] * gelu.astype(BF16)

    def phase(matmul_piece, act, slot, out_ref, lane_groups):
        units = [(tg, g0) for tg in lane_groups for g0 in range(0, blocks_per_half, 4)]
        matmul_piece(0, 1)
        for unit in units:
            gate_unit(act, slot, out_ref, *unit)

    def value_piece(lo, w_ref):
        def piece(idx, n_pieces):
            rows = d_model // n_pieces
            r0 = idx * rows
            acc_s[r0:r0 + rows, :] += jnp.dot(vt_ref[r0:r0 + rows, lo:lo + half], w_ref[...],
                                              preferred_element_type=F32)
        return piece

    def act_piece(lo, dst_ref):
        def piece(idx, n_pieces):
            rows = half // n_pieces
            r0 = idx * rows
            dst_ref[r0:r0 + rows, :] = jnp.dot(u_ref[lo + r0:lo + r0 + rows, :], ht_s[...],
                                               preferred_element_type=F32)
        return piece

    prev_b = (2 * p - 1) * blocks_per_half

    first = tuple(range(n_lane_groups // 2))
    second = tuple(range(n_lane_groups // 2, n_lane_groups))

    @pl.when(p < n_chunks)
    def _stage():
        stage_rows(prev_b, 0)
        stage_rows(2 * p * blocks_per_half, 1)

    @pl.when(p + 1 <= n_chunks)
    def _phase1():
        phase(value_piece(0, wga_s), act_s, 0, wgb_s, first)

    @pl.when(p + 2 <= n_chunks + 1)
    def _phase2():
        phase(act_piece(0, acta_s), act_s, 0, wgb_s, second)

    @pl.when(p + 3 <= n_chunks + 2)
    def _phase3():
        phase(value_piece(half, wgb_s), acta_s, 1, wga_s, first)

    @pl.when(p + 4 <= n_chunks + 3)
    def _phase4():
        phase(act_piece(half, act_s), acta_s, 1, wga_s, second)

    @pl.when(p == n_chunks)
    def _drain1():
        stage_rows(prev_b, 0)
        phase(value_piece(0, wga_s), act_s, 0, wgb_s, first + second)

    @pl.when(p >= n_chunks)
    def _drain2():
        out = acc_s[...] + jnp.dot(vt_ref[:, half:], wgb_s[...], preferred_element_type=F32)
        y = x_ref[0] + mod_ref[0, 5:6, :] * out.T
        if final_norm:
            y = y * lax.rsqrt(jnp.mean(y * y, axis=-1, keepdims=True) + RMS_EPS) * fw_ref[...]
        o_ref[0] = y


def _peer(x, mods, nw, wq_t, keys, u, v_t, layer, final_w, final_norm, tm=512, chunk=2048):
    b, l, d = x.shape
    tm = min(tm, l)
    n_chunks = N_EXPERTS // chunk
    half = chunk // 2
    kern = functools.partial(_peer_kernel, n_chunks=n_chunks, blocks_per_half=half // N_KEYS,
                             final_norm=final_norm)
    head_shape = (PEER_HEADS, N_KEYS, tm)
    once = pl.Buffered(1)
    return pl.pallas_call(
        kern,
        grid=(b, l // tm, n_chunks + 1),
        in_specs=[pl.BlockSpec((1, tm, d), lambda i, t, j: (i, t, 0)),
                  pl.BlockSpec((1, 8, d), lambda i, t, j: (i, 0, 0)),
                  pl.BlockSpec((1, d), lambda i, t, j: (0, 0)),
                  pl.BlockSpec((1, d), lambda i, t, j: (0, 0)),
                  pl.BlockSpec((None,) + wq_t.shape[1:], lambda i, t, j: (layer, 0, 0), pipeline_mode=once),
                  pl.BlockSpec((None,) + keys.shape[1:], lambda i, t, j: (layer, 0, 0, 0, 0),
                               pipeline_mode=once),
                  pl.BlockSpec((None, chunk, d), lambda i, t, j: (layer, jnp.minimum(j, n_chunks - 1), 0)),
                  pl.BlockSpec((None, d, chunk), lambda i, t, j: (layer, 0, jnp.maximum(j - 1, 0)))],
        out_specs=pl.BlockSpec((1, tm, d), lambda i, t, j: (i, t, 0)),
        out_shape=jax.ShapeDtypeStruct((b, l, d), F32),
        scratch_shapes=[pltpu.VMEM((d, tm), BF16),
                        pltpu.VMEM(head_shape, F32), pltpu.VMEM(head_shape, F32),
                        pltpu.VMEM(head_shape, BF16), pltpu.VMEM(head_shape, BF16),
                        pltpu.VMEM((d, tm), F32),
                        pltpu.VMEM((half, tm), F32), pltpu.VMEM((half, tm), F32),
                        pltpu.VMEM((half, tm), BF16), pltpu.VMEM((half, tm), BF16),
                        pltpu.VMEM((2, N_KEYS, tm), F32),
                        pltpu.VMEM((2, half // N_KEYS * PEER_HEADS, tm), F32),
                        pltpu.VMEM((2, half // N_KEYS * PEER_HEADS, tm), F32)],
        compiler_params=_params(("parallel", "parallel", "arbitrary")),
        name="peer",
    )(x, mods, nw, final_w, wq_t, keys, u, v_t)


def _block_diag(w):
    g, n, _ = w.shape
    out = jnp.zeros((g * n, g * n), w.dtype)
    for i in range(g):
        out = out.at[i * n:(i + 1) * n, i * n:(i + 1) * n].set(w[i])
    return out


def _reorder_w_in(w):
    d = w.shape[0]
    return jnp.concatenate([w[:, :REF_QKV], w[:, REF_ALPHA:REF_Z], w[:, REF_Z:REF_POOL],
                            w[:, REF_POOL:REF_FNET], w[:, REF_QKV:REF_ALPHA],
                            jnp.zeros((d, IN_PAD - REF_FNET), w.dtype)], axis=1).astype(BF16)


def _gate_columns(gz, gx):
    ba = jnp.concatenate([gz[:, :, 0:16], gx[:, :, 0:16]], axis=1)
    b, ltot, _ = ba.shape
    return ba.reshape(b, ltot, 4, DN_HEADS).transpose(0, 3, 1, 2)


def _mod_rows(mod, rows):
    d = mod.shape[1] // N_MOD
    m = mod.reshape(16, N_MOD, d)[rows]
    return jnp.concatenate([m, jnp.zeros((m.shape[0], 8 - N_MOD, d), m.dtype)], axis=1)


def kernel(x, c, ctx, c_ctx, w_mod, b_mod, norm1_w, norm2_w, w_in, conv_w, a_log, dt_bias,
           dn_norm_w, w_pool, pool_scale, w_fnet, w_out, w_query, sub_keys, expert_u, expert_v,
           final_norm_w):
    b, l, d = x.shape
    m = ctx.shape[1]
    depth = w_mod.shape[0]
    rows = l // GRID_W

    cs = jnp.concatenate([c, c_ctx[None, :], jnp.zeros((16 - b - 1, d), F32)], axis=0)
    mod_all = _modulation(cs, w_mod, b_mod)

    wq_t = jnp.swapaxes(w_query, 1, 2).astype(BF16)
    keys = sub_keys.astype(BF16)
    u_b = expert_u.astype(BF16)
    v_t = jnp.swapaxes(expert_v, 1, 2).astype(BF16)
    fw = final_norm_w[None, :]

    z = ctx
    for i in range(depth):
        update_ctx = i < depth - 1
        mods_x = _mod_rows(mod_all[i], jnp.arange(b))
        mods_z = _mod_rows(mod_all[i], jnp.full((b,), b))
        w_in_r = _reorder_w_in(w_in[i])
        nw1 = norm1_w[i][None, :]
        nw2 = norm2_w[i][None, :]

        px, gx = _inproj(x, mods_x, nw1, w_in_r)
        pz, gz = _inproj(z, mods_z, nw1, w_in_r)

        cw = conv_w[i].reshape(CONV_WIDTH, 3, DN_HEADS, DN_HEAD_DIM).transpose(2, 1, 0, 3)
        par = jnp.concatenate([a_log[i], dt_bias[i]], axis=0)
        par = jnp.broadcast_to(par.T[:, :, None], (DN_HEADS, 4, LANES))
        par = jnp.concatenate([par, jnp.zeros((DN_HEADS, 4, LANES), F32)], axis=1)
        dn_x, dn_z = _delta(pz, px, _gate_columns(gz, gx), cw, par, dn_norm_w[i][None, :])

        wp_bd = _block_diag(w_pool[i]).astype(BF16)
        wf_bd = _block_diag(w_fnet[i])
        w_out_b = w_out[i].astype(BF16)
        ps = pool_scale[i][None, :]

        fn_x = _fnet(px, _fnet_weights(wf_bd, l).astype(BF16))
        x = _mixout(x, dn_x, px, fn_x, l // rows, wp_bd, ps, w_out_b, mods_x)
        if update_ctx:
            fn_z = _fnet(pz, _fnet_weights(wf_bd, m).astype(BF16))
            z = _mixout(z, dn_z, pz, fn_z, m, wp_bd, ps, w_out_b, mods_z)

        x = _peer(x, mods_x, nw2, wq_t, keys, u_b, v_t, i, fw, not update_ctx)
        if update_ctx:
            z = _peer(z, mods_z, nw2, wq_t, keys, u_b, v_t, i, fw, False)

    return x
```

```python
import functools
import math

import numpy as np
import jax
import jax.numpy as jnp
from jax import lax
from jax.experimental import pallas as pl
from jax.experimental.pallas import tpu as pltpu

F32 = jnp.float32
BF16 = jnp.bfloat16
HIGHEST = lax.Precision.HIGHEST

D_MODEL = 1024
DN_HEADS = 4
DN_HEAD_DIM = 128
DN_WIDTH = DN_HEADS * DN_HEAD_DIM
POOL_WINDOWS = (2, 4, 8, 16)
POOL_WIDTH = 256
POOL_GROUP_DIM = 64
FNET_WIDTH = 256
FNET_GROUP_DIM = 64
CONV_WIDTH = 4
CHUNK = 64
GRID_W = 64
PEER_HEADS = 8
N_KEYS = 128
N_EXPERTS = N_KEYS * N_KEYS
PEER_TOPK = 16
N_MOD = 6
RMS_EPS = 1e-6
L2_EPS = 1e-6

COL_Q, COL_K, COL_V, COL_Z = 0, 512, 1024, 1536
COL_POOL, COL_FNET, COL_BA = 2048, 2304, 2560
IN_PAD = 2688
REF_QKV, REF_BETA, REF_ALPHA, REF_Z, REF_POOL, REF_FNET = 1536, 1544, 1552, 2064, 2320, 2576

LANES = 128
VMEM_LIMIT = 56 * 1024 * 1024


def _params(sem, flags=None):
    return pltpu.CompilerParams(dimension_semantics=sem, vmem_limit_bytes=VMEM_LIMIT, flags=flags)


def _mm(a, b):
    return jnp.dot(a.astype(BF16), b.astype(BF16), preferred_element_type=F32)


def _mm_nt(a, b):
    return lax.dot_general(a.astype(BF16), b.astype(BF16), (((1,), (1,)), ((), ())),
                           preferred_element_type=F32)


def _mm_inv(a, b):
    return _mm(a, b)


def _mm_f32(a, b):
    return jnp.dot(a, b, precision=HIGHEST, preferred_element_type=F32)


def _silu(x):
    return x * jax.nn.sigmoid(x)


def _mod_kernel(c_ref, w_ref, b_ref, o_ref):
    o_ref[0] = _mm_f32(_silu(c_ref[...]), w_ref[0]) + b_ref[0]


def _modulation(cs, w_mod, b_mod):
    depth, d, n = w_mod.shape
    tn = 1536
    return pl.pallas_call(
        _mod_kernel,
        grid=(depth, n // tn),
        in_specs=[pl.BlockSpec((16, d), lambda i, j: (0, 0)),
                  pl.BlockSpec((1, d, tn), lambda i, j: (i, 0, j)),
                  pl.BlockSpec((1, 1, tn), lambda i, j: (i, 0, j))],
        out_specs=pl.BlockSpec((1, 16, tn), lambda i, j: (i, 0, j)),
        out_shape=jax.ShapeDtypeStruct((depth, 16, n), F32),
        compiler_params=_params(("parallel", "parallel")),
        name="modulation",
    )(cs, w_mod, b_mod.reshape(depth, 1, n))


def _norm_mod(x, nw, shift, scale):
    y = x * lax.rsqrt(jnp.mean(x * x, axis=-1, keepdims=True) + RMS_EPS) * nw
    return y * (1.0 + scale) + shift


def _inproj_kernel(x_ref, mod_ref, nw_ref, w_ref, o_ref, g_ref):
    h = _norm_mod(x_ref[0], nw_ref[...], mod_ref[0, 0:1, :], mod_ref[0, 1:2, :])
    p = _mm(h, w_ref[...])
    o_ref[0] = p
    g_ref[0] = p[:, COL_BA:]


def _inproj(x, mods, nw, w):
    b, l, d = x.shape
    n = w.shape[1]
    tm = min(512, l)
    return pl.pallas_call(
        _inproj_kernel,
        grid=(b, l // tm),
        in_specs=[pl.BlockSpec((1, tm, d), lambda i, j: (i, j, 0)),
                  pl.BlockSpec((1, 8, d), lambda i, j: (i, 0, 0)),
                  pl.BlockSpec((1, d), lambda i, j: (0, 0)),
                  pl.BlockSpec((d, n), lambda i, j: (0, 0))],
        out_specs=[pl.BlockSpec((1, tm, n), lambda i, j: (i, j, 0)),
                   pl.BlockSpec((1, tm, n - COL_BA), lambda i, j: (i, j, 0))],
        out_shape=[jax.ShapeDtypeStruct((b, l, n), F32),
                   jax.ShapeDtypeStruct((b, l, n - COL_BA), F32)],
        compiler_params=_params(("parallel", "parallel")),
        name="inproj",
    )(x, mods, nw, w)


def _delta_kernel(qz_ref, kz_ref, vz_ref, zz_ref, qx_ref, kx_ref, vx_ref, zx_ref,
                  gate_ref, cw_ref, par_ref, dnw_ref, ox_ref, oz_ref,
                  q_s, k_s, v_s, gf_s, gb_s, bf_s, bb_s, o_s, pad_s, kw_s, c_s, qe_s, o0_s, dc_s,
                  *, m_len, l_len):
    ltot = m_len + l_len
    n_ctx = m_len // CHUNK
    n_tot = ltot // CHUNK
    off = CHUNK

    def conv_silu(src_ref, w, length):
        pad_s[off - 8:off, :] = jnp.zeros((8, LANES), F32)
        pad_s[off:off + length, :] = src_ref[0]
        pad_s[off + length:off + length + 8, :] = jnp.zeros((8, LANES), F32)
        y = (w[0:1, :] * pad_s[off - 2:off - 2 + length, :]
             + w[1:2, :] * pad_s[off - 1:off - 1 + length, :]
             + w[2:3, :] * pad_s[off:off + length, :]
             + w[3:4, :] * pad_s[off + 1:off + 1 + length, :])
        return _silu(y)

    def l2norm(t):
        return t * lax.rsqrt(jnp.sum(t * t, axis=-1, keepdims=True) + L2_EPS)

    for (src_q, src_k, src_v, start, length) in ((qz_ref, kz_ref, vz_ref, 0, m_len),
                                                 (qx_ref, kx_ref, vx_ref, m_len, l_len)):
        q_s[start:start + length, :] = l2norm(conv_silu(src_q, cw_ref[0, 0], length)) * (DN_HEAD_DIM ** -0.5)
        k_s[start:start + length, :] = l2norm(conv_silu(src_k, cw_ref[0, 1], length))
        v_s[start:start + length, :] = conv_silu(src_v, cw_ref[0, 2], length)

    gt = gate_ref[0, 0]
    par = par_ref[0]
    shape = (ltot, LANES)
    def softplus(t):
        return jnp.maximum(t, 0.0) + jnp.log1p(jnp.exp(-jnp.abs(t)))

    beta = jax.nn.sigmoid(gt)
    a_log = jnp.concatenate([par[0:1, 0:2], par[0:1, 0:1], par[1:2, 0:1]], axis=1)
    dt_bias = jnp.concatenate([par[0:1, 0:2], par[2:3, 0:1], par[3:4, 0:1]], axis=1)
    log_decay = -jnp.exp(a_log) * softplus(gt + dt_bias)
    bf_s[...] = jnp.broadcast_to(beta[:, 0:1], shape)
    bb_s[...] = jnp.broadcast_to(beta[:, 1:2], shape)
    gf_s[...] = jnp.broadcast_to(log_decay[:, 2:3], shape)
    gb_s[...] = jnp.broadcast_to(log_decay[:, 3:4], shape)

    pos = lax.broadcasted_iota(jnp.int32, shape, 0) & (CHUNK - 1)
    pad_s[0:off, :] = jnp.zeros((off, LANES), F32)
    pad_s[off + ltot:off + ltot + off, :] = jnp.zeros((off, LANES), F32)
    step = 1
    while step < CHUNK:
        pad_s[off:off + ltot, :] = gf_s[...]
        gf_s[...] = gf_s[...] + jnp.where(pos >= step, pad_s[off - step:off - step + ltot, :], 0.0)
        pad_s[off:off + ltot, :] = gb_s[...]
        gb_s[...] = gb_s[...] + jnp.where(pos < CHUNK - step, pad_s[off + step:off + step + ltot, :], 0.0)
        step *= 2

    ri = lax.broadcasted_iota(jnp.int32, (CHUNK, CHUNK), 0)
    ci = lax.broadcasted_iota(jnp.int32, (CHUNK, CHUNK), 1)
    eye = (ri == ci).astype(F32)
    hd = DN_HEAD_DIM
    pair_mask = {1 << b: jnp.logical_and((ri >> (b + 1)) == (ci >> (b + 1)), (ri >> b) != (ci >> b))
                 for b in range(6)}

    group = max(g for g in range(1, 13) if n_tot % g == 0)

    def stage1(g, carry):
        chains = [(g * group + c, d) for c in range(group) for d in (0, 1)]
        both = lambda fn: [fn(n, d) for (n, d) in chains]
        each = lambda fn, *cols: [fn(*vals) for vals in zip(*cols)]
        rows = lambda n: pl.ds(pl.multiple_of(n * CHUNK, CHUNK), CHUNK)
        incl = both(lambda n, d: (ri >= ci) if d == 0 else (ri <= ci))
        strict = both(lambda n, d: (ri > ci) if d == 0 else (ri < ci))
        q = both(lambda n, d: q_s[rows(n), :])
        k = both(lambda n, d: k_s[rows(n), :])
        v = both(lambda n, d: v_s[rows(n), :])
        gi = both(lambda n, d: (gf_s if d == 0 else gb_s)[rows(n), :])
        bi = both(lambda n, d: (bf_s if d == 0 else bb_s)[rows(n), :])
        diff = each(lambda g_: g_[:, 0:CHUNK] - g_.T[0:CHUNK, :], gi)
        decay = each(lambda m, df: jnp.where(m, jnp.exp(jnp.where(m, df, 0.0)), 0.0), incl, diff)
        kb = each(lambda k_, b_: k_ * b_, k, bi)
        kk = each(_mm_nt, kb, k)
        low = each(lambda m, kk_, dec: jnp.where(m, kk_ * dec, 0.0), strict, kk, decay)
        inv = [eye - jnp.where(pair_mask[1], l_, 0.0) for l_ in low]
        size = 2
        while size < CHUNK:
            coupling = [jnp.where(pair_mask[size], l_, 0.0) for l_ in low]
            left = each(lambda a_, b_: _mm_inv(a_, b_), inv, coupling)
            corr = each(lambda a_, b_: _mm_inv(a_, b_), left, inv)
            inv = each(lambda t, u_: t - u_, inv, corr)
            size *= 2
        eg = [jnp.exp(g_) for g_ in gi]
        rhs = each(lambda kb_, e_, v_, b_: jnp.concatenate([kb_ * e_, v_ * b_], axis=1), kb, eg, v, bi)
        sol = each(_mm, inv, rhs)
        qk = each(_mm_nt, q, k)
        intra = each(lambda m, qk_, dec: jnp.where(m, qk_ * dec, 0.0), incl, qk, decay)
        g_last = [g_[CHUNK - 1:CHUNK, :] if d == 0 else g_[0:1, :] for g_, (_, d) in zip(gi, chains)]
        k_tail_t = each(lambda k_, gl, g_: (k_ * jnp.exp(gl - g_)).T, k, g_last, gi)
        iwu = each(_mm, intra, sol)
        kwu = each(_mm, k_tail_t, sol)
        for idx, (n, d) in enumerate(chains):
            sk = pl.ds(pl.multiple_of(n * hd, hd), hd)
            qe_s[d, rows(n), :] = (q[idx] * eg[idx] - iwu[idx][:, 0:hd]).astype(BF16)
            o0_s[d, rows(n), :] = iwu[idx][:, hd:]
            kw_s[d, sk, :] = kwu[idx][:, 0:hd].astype(BF16)
            c_s[d, sk, :] = kwu[idx][:, hd:]
            dc_s[d, pl.ds(pl.multiple_of(n * 8, 8), 8), :] = jnp.broadcast_to(jnp.exp(g_last[idx]), (8, LANES))
        return carry

    lax.fori_loop(0, n_tot // group, stage1, 0)

    def advance(direction, n, state):
        sl = pl.ds(pl.multiple_of(n * CHUNK, CHUNK), CHUNK)
        sk = pl.ds(pl.multiple_of(n * hd, hd), hd)
        sb = state.astype(BF16)
        o_s[direction, sl, :] = (jnp.dot(qe_s[direction, sl, :], sb, preferred_element_type=F32)
                                 + o0_s[direction, sl, :])
        dc = dc_s[direction, pl.ds(pl.multiple_of(n * 8, 8), 1), :]
        return (state * dc - jnp.dot(kw_s[direction, sk, :], sb, preferred_element_type=F32)
                + c_s[direction, sk, :])

    def stage2(t, states):
        n_b = jnp.where(t < n_ctx, n_ctx - 1 - t, n_tot + n_ctx - 1 - t)
        return advance(0, t, states[0]), advance(1, n_b, states[1])

    zero = jnp.zeros((hd, hd), F32)
    lax.fori_loop(0, n_tot, stage2, (zero, zero))

    o = o_s[0] + o_s[1]
    o = o * lax.rsqrt(jnp.mean(o * o, axis=-1, keepdims=True) + RMS_EPS) * dnw_ref[...]
    oz_ref[0] = o[0:m_len, :] * _silu(zz_ref[0])
    ox_ref[0] = o[m_len:, :] * _silu(zx_ref[0])


def _delta(pz, px, gates, conv_w, par, dn_norm_w):
    b, m_len, _ = pz.shape
    l_len = px.shape[1]
    ltot = m_len + l_len
    n_tot = ltot // CHUNK
    hb = LANES

    def col(c0, length):
        return pl.BlockSpec((1, length, hb), lambda i, h: (i, 0, c0 // hb + h))

    scratch = [pltpu.VMEM((ltot, LANES), F32) for _ in range(7)]
    scratch += [pltpu.VMEM((2, ltot, LANES), F32),
                pltpu.VMEM((ltot + 2 * CHUNK, LANES), F32),
                pltpu.VMEM((2, n_tot * DN_HEAD_DIM, LANES), BF16),
                pltpu.VMEM((2, n_tot * DN_HEAD_DIM, LANES), F32),
                pltpu.VMEM((2, ltot, LANES), BF16),
                pltpu.VMEM((2, ltot, LANES), F32),
                pltpu.VMEM((2, n_tot * 8, LANES), F32)]
    return pl.pallas_call(
        functools.partial(_delta_kernel, m_len=m_len, l_len=l_len),
        grid=(b, DN_HEADS),
        in_specs=[col(COL_Q, m_len), col(COL_K, m_len), col(COL_V, m_len), col(COL_Z, m_len),
                  col(COL_Q, l_len), col(COL_K, l_len), col(COL_V, l_len), col(COL_Z, l_len),
                  pl.BlockSpec((1, 1, ltot, 4), lambda i, h: (i, h, 0, 0)),
                  pl.BlockSpec((1, 3, CONV_WIDTH, hb), lambda i, h: (h, 0, 0, 0)),
                  pl.BlockSpec((1, 8, hb), lambda i, h: (h, 0, 0)),
                  pl.BlockSpec((1, hb), lambda i, h: (0, 0))],
        out_specs=[pl.BlockSpec((1, l_len, hb), lambda i, h: (i, 0, h)),
                   pl.BlockSpec((1, m_len, hb), lambda i, h: (i, 0, h))],
        out_shape=[jax.ShapeDtypeStruct((b, l_len, DN_WIDTH), F32),
                   jax.ShapeDtypeStruct((b, m_len, DN_WIDTH), F32)],
        scratch_shapes=scratch,
        compiler_params=_params(("parallel", "parallel")),
        name="delta",
    )(pz, pz, pz, pz, px, px, px, px, gates, conv_w, par, dn_norm_w)


def _fnet_weight_kernel(c_ref, s_ref, w_ref, o_ref, *, scale):
    w = w_ref[...]
    o_ref[:, 0:FNET_WIDTH] = _mm_f32(c_ref[...], w) * scale
    o_ref[:, FNET_WIDTH:] = _mm_f32(s_ref[...], w) * scale


def _fnet_weights(w_bd, seq_len):
    n = FNET_GROUP_DIM
    ang = 2.0 * np.pi * ((np.arange(n)[:, None] * np.arange(n)[None, :]) % n) / n
    eye4 = np.eye(FNET_WIDTH // n)
    c_bd = jnp.asarray(np.kron(eye4, np.cos(ang)), F32)
    s_bd = jnp.asarray(np.kron(eye4, np.sin(ang)), F32)
    return pl.pallas_call(
        functools.partial(_fnet_weight_kernel, scale=float(1.0 / math.sqrt(seq_len * n))),
        out_shape=jax.ShapeDtypeStruct((FNET_WIDTH, 2 * FNET_WIDTH), F32),
        name="fnet_weights",
    )(c_bd, s_bd, w_bd)


def _fnet_kernel(x_ref, w_ref, c_ref, s_ref, o_ref):
    xw = _mm(x_ref[0], w_ref[...])
    o_ref[0] = _mm(c_ref[...], xw[:, 0:FNET_WIDTH]) + _mm(s_ref[...], xw[:, FNET_WIDTH:])


def _dft_mats(seq_len):
    idx = (np.arange(seq_len)[:, None] * np.arange(seq_len)[None, :]) % seq_len
    ang = 2.0 * np.pi * idx / seq_len
    return jnp.asarray(np.cos(ang), F32).astype(BF16), jnp.asarray(-np.sin(ang), F32).astype(BF16)


def _fnet(p, wcs):
    b, l, _ = p.shape
    tm = min(512, l)
    cm, sm = _dft_mats(l)
    return pl.pallas_call(
        _fnet_kernel,
        grid=(l // tm, b),
        in_specs=[pl.BlockSpec((1, l, FNET_WIDTH), lambda i, j: (j, 0, COL_FNET // FNET_WIDTH)),
                  pl.BlockSpec((FNET_WIDTH, 2 * FNET_WIDTH), lambda i, j: (0, 0)),
                  pl.BlockSpec((tm, l), lambda i, j: (i, 0)),
                  pl.BlockSpec((tm, l), lambda i, j: (i, 0))],
        out_specs=pl.BlockSpec((1, tm, FNET_WIDTH), lambda i, j: (j, i, 0)),
        out_shape=jax.ShapeDtypeStruct((b, l, FNET_WIDTH), F32),
        compiler_params=_params(("parallel", "parallel")),
        name="fnet",
    )(p, wcs, cm, sm)


def _mixout_kernel(x_ref, dn_ref, p_ref, f_ref, band_ref, icnt_ref, wp_ref, ps_ref, wo_ref,
                   mod_ref, o_ref):
    xp = p_ref[0]
    group = lax.broadcasted_iota(jnp.int32, xp.shape, 1) // POOL_GROUP_DIM
    win_sum = jnp.zeros(xp.shape, F32)
    for g in range(len(POOL_WINDOWS)):
        win_sum = win_sum + _mm(band_ref[g], jnp.where(group == g, xp, 0.0))
    y = win_sum * icnt_ref[...] - xp
    pool = _mm(y, wp_ref[...]) * ps_ref[...]
    out = (_mm(dn_ref[0], wo_ref[0:DN_WIDTH, :])
           + _mm(pool, wo_ref[DN_WIDTH:DN_WIDTH + POOL_WIDTH, :])
           + _mm(f_ref[0], wo_ref[DN_WIDTH + POOL_WIDTH:, :]))
    o_ref[0] = x_ref[0] + mod_ref[0, 2:3, :] * out


def _pool_consts(tm, seg):
    pos = np.arange(tm) % seg
    seg_id = np.arange(tm) // seg
    band = np.zeros((len(POOL_WINDOWS), tm, tm), np.float32)
    icnt = np.zeros((tm, POOL_WIDTH), np.float32)
    for g, w in enumerate(POOL_WINDOWS):
        lo = np.clip(pos - w // 2, 0, seg)
        hi = np.clip(pos + w - w // 2, 0, seg)
        j = np.arange(tm)
        inside = (seg_id[:, None] == seg_id[None, :]) & (pos[None, :] >= lo[:, None]) & (pos[None, :] < hi[:, None])
        band[g] = inside.astype(np.float32)
        icnt[:, g * POOL_GROUP_DIM:(g + 1) * POOL_GROUP_DIM] = (1.0 / (hi - lo))[:, None]
    return jnp.asarray(band).astype(BF16), jnp.asarray(icnt)


def _mixout(x, dn, p, fn, seg, wp_bd, pool_scale, w_out, mods):
    b, l, d = x.shape
    tm = min(256, l)
    band, icnt = _pool_consts(tm, min(seg, tm))
    if seg > tm:
        raise ValueError("pooling segment longer than the token tile")
    return pl.pallas_call(
        _mixout_kernel,
        grid=(b, l // tm),
        in_specs=[pl.BlockSpec((1, tm, d), lambda i, j: (i, j, 0)),
                  pl.BlockSpec((1, tm, DN_WIDTH), lambda i, j: (i, j, 0)),
                  pl.BlockSpec((1, tm, POOL_WIDTH), lambda i, j: (i, j, COL_POOL // POOL_WIDTH)),
                  pl.BlockSpec((1, tm, FNET_WIDTH), lambda i, j: (i, j, 0)),
                  pl.BlockSpec((len(POOL_WINDOWS), tm, tm), lambda i, j: (0, 0, 0)),
                  pl.BlockSpec((tm, POOL_WIDTH), lambda i, j: (0, 0)),
                  pl.BlockSpec((POOL_WIDTH, POOL_WIDTH), lambda i, j: (0, 0)),
                  pl.BlockSpec((1, POOL_WIDTH), lambda i, j: (0, 0)),
                  pl.BlockSpec((d, d), lambda i, j: (0, 0)),
                  pl.BlockSpec((1, 8, d), lambda i, j: (i, 0, 0))],
        out_specs=pl.BlockSpec((1, tm, d), lambda i, j: (i, j, 0)),
        out_shape=jax.ShapeDtypeStruct((b, l, d), F32),
        compiler_params=_params(("parallel", "parallel")),
        name="mixout",
    )(x, dn, p, fn, band, icnt, wp_bd, pool_scale, w_out, mods)


def _peer_kernel(x_ref, mod_ref, nw_ref, fw_ref, wq_ref, keys_ref, u_ref, vt_ref, o_ref,
                 ht_s, c1_s, e1_s, r2_s, e2_s, acc_s, act_s, acta_s, wga_s, wgb_s, sc_s, rc_s, re_s,
                 *, n_chunks, blocks_per_half, final_norm):
    p = pl.program_id(2)
    neg_inf = float("-inf")
    half = blocks_per_half * N_KEYS
    n_lane_groups = x_ref.shape[1] // LANES
    d_model = x_ref.shape[2]

    @pl.when(p == 0)
    def _route():
        h = _norm_mod(x_ref[0], nw_ref[...], mod_ref[0, 3:4, :], mod_ref[0, 4:5, :])
        ht_s[...] = h.T.astype(BF16)

        marker = 2.0 ** 100

        def extract(s):
            vals = []
            for r in range(PEER_TOPK):
                mx = jnp.max(s, axis=0, keepdims=True)
                vals.append(mx)
                s = jnp.where(s == mx, -marker * (r + 1), s)
            rank = jnp.where(s < -0.5 * marker, s * (-1.0 / marker) - 1.0, 99.0)
            return vals, rank

        pairs = [(a, b) for a in range(PEER_TOPK) for b in range(PEER_TOPK)
                 if (a + 1) * (b + 1) <= PEER_TOPK]
        n_pad = (-len(pairs)) % 8

        for hd in range(PEER_HEADS):
            qt = jnp.dot(wq_ref[hd * 2 * N_KEYS:(hd + 1) * 2 * N_KEYS, :], ht_s[...],
                         preferred_element_type=F32)
            sc_s[0] = _mm(keys_ref[hd, 0], qt[0:N_KEYS, :])
            sc_s[1] = _mm(keys_ref[hd, 1], qt[N_KEYS:, :])
            for tg in range(n_lane_groups):
                ln = slice(tg * LANES, (tg + 1) * LANES)
                s1 = sc_s[0, :, ln]
                s2 = sc_s[1, :, ln]
                t1, rank1 = extract(s1)
                t2, rank2 = extract(s2)
                cands = [t1[a] + t2[b] for a, b in pairs]
                cand = jnp.concatenate(cands + [jnp.full_like(cands[0], neg_inf)] * n_pad, axis=0)
                work = cand
                thr = jnp.full_like(cands[0], neg_inf)
                found = jnp.zeros_like(cands[0])
                for _ in range(PEER_TOPK):
                    mx = jnp.max(work, axis=0, keepdims=True)
                    cnt = jnp.sum(jnp.where(cand >= mx, 1.0, 0.0), axis=0, keepdims=True)
                    hit = jnp.where(cnt >= PEER_TOPK, 1.0 - found, 0.0)
                    thr = jnp.where(hit > 0.0, mx, thr)
                    found = jnp.maximum(found, hit)
                    work = jnp.where(work == mx, neg_inf, work)
                z = jnp.sum(jnp.where(cand >= thr, jnp.exp(cand - cands[0]), 0.0), axis=0, keepdims=True)
                cnt1 = jnp.zeros(s1.shape, F32)
                for a in range(PEER_TOPK):
                    partners = jnp.zeros_like(thr)
                    for idx, (pa, _) in enumerate(pairs):
                        if pa == a:
                            partners = partners + jnp.where(cands[idx] >= thr, 1.0, 0.0)
                    cnt1 = jnp.where(rank1 == float(a), partners, cnt1)
                c1_s[hd, :, ln] = cnt1
                e1_s[hd, :, ln] = jnp.exp(s1 - t1[0]) * (1.0 / z)
                r2_s[hd, :, ln] = rank2.astype(BF16)
                e2_s[hd, :, ln] = jnp.exp(s2 - t2[0]).astype(BF16)
        acc_s[...] = jnp.zeros(acc_s.shape, F32)
        act_s[...] = jnp.zeros(act_s.shape, F32)
        wga_s[...] = jnp.zeros(wga_s.shape, BF16)

    def row_bf16(row):
        tile = jnp.broadcast_to(row, (16, LANES)).astype(BF16)
        return jnp.concatenate([tile] * (N_KEYS // 16), axis=0)

    def stage_rows(first_block, slot):
        for ii in range(blocks_per_half):
            i = jnp.maximum(first_block + ii, 0)
            for hd in range(PEER_HEADS):
                k = ii * PEER_HEADS + hd
                rc_s[slot, k:k + 1, :] = c1_s[hd, pl.ds(i, 1), :]
                re_s[slot, k:k + 1, :] = e1_s[hd, pl.ds(i, 1), :]

    def gate_unit(act, slot, out_ref, tg, g0):
        ln = slice(tg * LANES, (tg + 1) * LANES)
        blocks = list(range(g0, min(g0 + 4, blocks_per_half)))
        gates = [jnp.zeros((N_KEYS, LANES), BF16) for _ in blocks]
        for hd in range(PEER_HEADS):
            r2 = r2_s[hd, :, ln]
            e2 = e2_s[hd, :, ln]
            for n, ii in enumerate(blocks):
                k = ii * PEER_HEADS + hd
                c1row = row_bf16(rc_s[slot, k:k + 1, ln])
                e1row = row_bf16(re_s[slot, k:k + 1, ln])
                gates[n] = gates[n] + jnp.minimum(jnp.maximum(c1row - r2, 0.0), e1row) * e2
        for n, ii in enumerate(blocks):
            a = act[ii * N_KEYS:(ii + 1) * N_KEYS, ln]
            gelu = 0.5 * a * (1.0 + lax.erf(a * (2.0 ** -0.5)))
            out_ref[ii * N_KEYS:(ii + 1) * N_KEYS, ln] = gates[n] * gelu.astype(BF16)

    def phase(matmul_piece, act, slot, out_ref, lane_groups):
        units = [(tg, g0) for tg in lane_groups for g0 in range(0, blocks_per_half, 4)]
        matmul_piece(0, 1)
        for unit in units:
            gate_unit(act, slot, out_ref, *unit)

    def value_piece(lo, w_ref):
        def piece(idx, n_pieces):
            rows = d_model // n_pieces
            r0 = idx * rows
            acc_s[r0:r0 + rows, :] += jnp.dot(vt_ref[r0:r0 + rows, lo:lo + half], w_ref[...],
                                              preferred_element_type=F32)
        return piece

    def act_piece(lo, dst_ref):
        def piece(idx, n_pieces):
            rows = half // n_pieces
            r0 = idx * rows
            dst_ref[r0:r0 + rows, :] = jnp.dot(u_ref[lo + r0:lo + r0 + rows, :], ht_s[...],
                                               preferred_element_type=F32)
        return piece

    prev_b = (2 * p - 1) * blocks_per_half

    first = tuple(range(n_lane_groups // 2))
    second = tuple(range(n_lane_groups // 2, n_lane_groups))

    @pl.when(p < n_chunks)
    def _stage():
        stage_rows(prev_b, 0)
        stage_rows(2 * p * blocks_per_half, 1)

    @pl.when(p + 1 <= n_chunks)
    def _phase1():
        phase(value_piece(0, wga_s), act_s, 0, wgb_s, first)

    @pl.when(p + 2 <= n_chunks + 1)
    def _phase2():
        phase(act_piece(0, acta_s), act_s, 0, wgb_s, second)

    @pl.when(p + 3 <= n_chunks + 2)
    def _phase3():
        phase(value_piece(half, wgb_s), acta_s, 1, wga_s, first)

    @pl.when(p + 4 <= n_chunks + 3)
    def _phase4():
        phase(act_piece(half, act_s), acta_s, 1, wga_s, second)

    @pl.when(p == n_chunks)
    def _drain1():
        stage_rows(prev_b, 0)
        phase(value_piece(0, wga_s), act_s, 0, wgb_s, first + second)

    @pl.when(p >= n_chunks)
    def _drain2():
        out = acc_s[...] + jnp.dot(vt_ref[:, half:], wgb_s[...], preferred_element_type=F32)
        y = x_ref[0] + mod_ref[0, 5:6, :] * out.T
        if final_norm:
            y = y * lax.rsqrt(jnp.mean(y * y, axis=-1, keepdims=True) + RMS_EPS) * fw_ref[...]
        o_ref[0] = y


def _peer(x, mods, nw, wq_t, keys, u, v_t, layer, final_w, final_norm, tm=512, chunk=2048):
    b, l, d = x.shape
    tm = min(tm, l)
    n_chunks = N_EXPERTS // chunk
    half = chunk // 2
    kern = functools.partial(_peer_kernel, n_chunks=n_chunks, blocks_per_half=half // N_KEYS,
                             final_norm=final_norm)
    head_shape = (PEER_HEADS, N_KEYS, tm)
    once = pl.Buffered(1)
    return pl.pallas_call(
        kern,
        grid=(b, l // tm, n_chunks + 1),
        in_specs=[pl.BlockSpec((1, tm, d), lambda i, t, j: (i, t, 0)),
                  pl.BlockSpec((1, 8, d), lambda i, t, j: (i, 0, 0)),
                  pl.BlockSpec((1, d), lambda i, t, j: (0, 0)),
                  pl.BlockSpec((1, d), lambda i, t, j: (0, 0)),
                  pl.BlockSpec((None,) + wq_t.shape[1:], lambda i, t, j: (layer, 0, 0), pipeline_mode=once),
                  pl.BlockSpec((None,) + keys.shape[1:], lambda i, t, j: (layer, 0, 0, 0, 0),
                               pipeline_mode=once),
                  pl.BlockSpec((None, chunk, d), lambda i, t, j: (layer, jnp.minimum(j, n_chunks - 1), 0)),
                  pl.BlockSpec((None, d, chunk), lambda i, t, j: (layer, 0, jnp.maximum(j - 1, 0)))],
        out_specs=pl.BlockSpec((1, tm, d), lambda i, t, j: (i, t, 0)),
        out_shape=jax.ShapeDtypeStruct((b, l, d), F32),
        scratch_shapes=[pltpu.VMEM((d, tm), BF16),
                        pltpu.VMEM(head_shape, F32), pltpu.VMEM(head_shape, F32),
                        pltpu.VMEM(head_shape, BF16), pltpu.VMEM(head_shape, BF16),
                        pltpu.VMEM((d, tm), F32),
                        pltpu.VMEM((half, tm), F32), pltpu.VMEM((half, tm), F32),
                        pltpu.VMEM((half, tm), BF16), pltpu.VMEM((half, tm), BF16),
                        pltpu.VMEM((2, N_KEYS, tm), F32),
                        pltpu.VMEM((2, half // N_KEYS * PEER_HEADS, tm), F32),
                        pltpu.VMEM((2, half // N_KEYS * PEER_HEADS, tm), F32)],
        compiler_params=_params(("parallel", "parallel", "arbitrary")),
        name="peer",
    )(x, mods, nw, final_w, wq_t, keys, u, v_t)


def _block_diag(w):
    g, n, _ = w.shape
    out = jnp.zeros((g * n, g * n), w.dtype)
    for i in range(g):
        out = out.at[i * n:(i + 1) * n, i * n:(i + 1) * n].set(w[i])
    return out


def _reorder_w_in(w):
    d = w.shape[0]
    return jnp.concatenate([w[:, :REF_QKV], w[:, REF_ALPHA:REF_Z], w[:, REF_Z:REF_POOL],
                            w[:, REF_POOL:REF_FNET], w[:, REF_QKV:REF_ALPHA],
                            jnp.zeros((d, IN_PAD - REF_FNET), w.dtype)], axis=1).astype(BF16)


def _gate_columns(gz, gx):
    ba = jnp.concatenate([gz[:, :, 0:16], gx[:, :, 0:16]], axis=1)
    b, ltot, _ = ba.shape
    return ba.reshape(b, ltot, 4, DN_HEADS).transpose(0, 3, 1, 2)


def _mod_rows(mod, rows):
    d = mod.shape[1] // N_MOD
    m = mod.reshape(16, N_MOD, d)[rows]
    return jnp.concatenate([m, jnp.zeros((m.shape[0], 8 - N_MOD, d), m.dtype)], axis=1)


def kernel(x, c, ctx, c_ctx, w_mod, b_mod, norm1_w, norm2_w, w_in, conv_w, a_log, dt_bias,
           dn_norm_w, w_pool, pool_scale, w_fnet, w_out, w_query, sub_keys, expert_u, expert_v,
           final_norm_w):
    b, l, d = x.shape
    m = ctx.shape[1]
    depth = w_mod.shape[0]
    rows = l // GRID_W

    cs = jnp.concatenate([c, c_ctx[None, :], jnp.zeros((16 - b - 1, d), F32)], axis=0)
    mod_all = _modulation(cs, w_mod, b_mod)

    wq_t = jnp.swapaxes(w_query, 1, 2).astype(BF16)
    keys = sub_keys.astype(BF16)
    u_b = expert_u.astype(BF16)
    v_t = jnp.swapaxes(expert_v, 1, 2).astype(BF16)
    fw = final_norm_w[None, :]

    z = ctx
    for i in range(depth):
        update_ctx = i < depth - 1
        mods_x = _mod_rows(mod_all[i], jnp.arange(b))
        mods_z = _mod_rows(mod_all[i], jnp.full((b,), b))
        w_in_r = _reorder_w_in(w_in[i])
        nw1 = norm1_w[i][None, :]
        nw2 = norm2_w[i][None, :]

        px, gx = _inproj(x, mods_x, nw1, w_in_r)
        pz, gz = _inproj(z, mods_z, nw1, w_in_r)

        cw = conv_w[i].reshape(CONV_WIDTH, 3, DN_HEADS, DN_HEAD_DIM).transpose(2, 1, 0, 3)
        par = jnp.concatenate([a_log[i], dt_bias[i]], axis=0)
        par = jnp.broadcast_to(par.T[:, :, None], (DN_HEADS, 4, LANES))
        par = jnp.concatenate([par, jnp.zeros((DN_HEADS, 4, LANES), F32)], axis=1)
        dn_x, dn_z = _delta(pz, px, _gate_columns(gz, gx), cw, par, dn_norm_w[i][None, :])

        wp_bd = _block_diag(w_pool[i]).astype(BF16)
        wf_bd = _block_diag(w_fnet[i])
        w_out_b = w_out[i].astype(BF16)
        ps = pool_scale[i][None, :]

        fn_x = _fnet(px, _fnet_weights(wf_bd, l).astype(BF16))
        x = _mixout(x, dn_x, px, fn_x, l // rows, wp_bd, ps, w_out_b, mods_x)
        if update_ctx:
            fn_z = _fnet(pz, _fnet_weights(wf_bd, m).astype(BF16))
            z = _mixout(z, dn_z, pz, fn_z, m, wp_bd, ps, w_out_b, mods_z)

        x = _peer(x, mods_x, nw2, wq_t, keys, u_b, v_t, i, fw, not update_ctx)
        if update_ctx:
            z = _peer(z, mods_z, nw2, wq_t, keys, u_b, v_t, i, fw, False)

    return x
```

```python
import functools
import math

import numpy as np
import jax
import jax.numpy as jnp
from jax import lax
from jax.experimental import pallas as pl
from jax.experimental.pallas import tpu as pltpu

F32 = jnp.float32
BF16 = jnp.bfloat16
HIGHEST = lax.Precision.HIGHEST

D_MODEL = 1024
DN_HEADS = 4
DN_HEAD_DIM = 128
DN_WIDTH = DN_HEADS * DN_HEAD_DIM
POOL_WINDOWS = (2, 4, 8, 16)
POOL_WIDTH = 256
POOL_GROUP_DIM = 64
FNET_WIDTH = 256
FNET_GROUP_DIM = 64
CONV_WIDTH = 4
CHUNK = 64
GRID_W = 64
PEER_HEADS = 8
N_KEYS = 128
N_EXPERTS = N_KEYS * N_KEYS
PEER_TOPK = 16
N_MOD = 6
RMS_EPS = 1e-6
L2_EPS = 1e-6

COL_Q, COL_K, COL_V, COL_Z = 0, 512, 1024, 1536
COL_POOL, COL_FNET, COL_BA = 2048, 2304, 2560
IN_PAD = 2688
REF_QKV, REF_BETA, REF_ALPHA, REF_Z, REF_POOL, REF_FNET = 1536, 1544, 1552, 2064, 2320, 2576

LANES = 128
VMEM_LIMIT = 56 * 1024 * 1024


def _params(sem, flags=None):
    return pltpu.CompilerParams(dimension_semantics=sem, vmem_limit_bytes=VMEM_LIMIT, flags=flags)


def _mm(a, b):
    return jnp.dot(a.astype(BF16), b.astype(BF16), preferred_element_type=F32)


def _mm_nt(a, b):
    return lax.dot_general(a.astype(BF16), b.astype(BF16), (((1,), (1,)), ((), ())),
                           preferred_element_type=F32)


def _mm_inv(a, b):
    return _mm(a, b)


def _mm_f32(a, b):
    return jnp.dot(a, b, precision=HIGHEST, preferred_element_type=F32)


def _silu(x):
    return x * jax.nn.sigmoid(x)


def _mod_kernel(c_ref, w_ref, b_ref, o_ref):
    o_ref[0] = _mm_f32(_silu(c_ref[...]), w_ref[0]) + b_ref[0]


def _modulation(cs, w_mod, b_mod):
    depth, d, n = w_mod.shape
    tn = 1536
    return pl.pallas_call(
        _mod_kernel,
        grid=(depth, n // tn),
        in_specs=[pl.BlockSpec((16, d), lambda i, j: (0, 0)),
                  pl.BlockSpec((1, d, tn), lambda i, j: (i, 0, j)),
                  pl.BlockSpec((1, 1, tn), lambda i, j: (i, 0, j))],
        out_specs=pl.BlockSpec((1, 16, tn), lambda i, j: (i, 0, j)),
        out_shape=jax.ShapeDtypeStruct((depth, 16, n), F32),
        compiler_params=_params(("parallel", "parallel")),
        name="modulation",
    )(cs, w_mod, b_mod.reshape(depth, 1, n))


def _norm_mod(x, nw, shift, scale):
    y = x * lax.rsqrt(jnp.mean(x * x, axis=-1, keepdims=True) + RMS_EPS) * nw
    return y * (1.0 + scale) + shift


def _inproj_kernel(x_ref, mod_ref, nw_ref, w_ref, o_ref, g_ref):
    h = _norm_mod(x_ref[0], nw_ref[...], mod_ref[0, 0:1, :], mod_ref[0, 1:2, :])
    p = _mm(h, w_ref[...])
    o_ref[0] = p
    g_ref[0] = p[:, COL_BA:]


def _inproj(x, mods, nw, w):
    b, l, d = x.shape
    n = w.shape[1]
    tm = min(512, l)
    return pl.pallas_call(
        _inproj_kernel,
        grid=(b, l // tm),
        in_specs=[pl.BlockSpec((1, tm, d), lambda i, j: (i, j, 0)),
                  pl.BlockSpec((1, 8, d), lambda i, j: (i, 0, 0)),
                  pl.BlockSpec((1, d), lambda i, j: (0, 0)),
                  pl.BlockSpec((d, n), lambda i, j: (0, 0))],
        out_specs=[pl.BlockSpec((1, tm, n), lambda i, j: (i, j, 0)),
                   pl.BlockSpec((1, tm, n - COL_BA), lambda i, j: (i, j, 0))],
        out_shape=[jax.ShapeDtypeStruct((b, l, n), F32),
                   jax.ShapeDtypeStruct((b, l, n - COL_BA), F32)],
        compiler_params=_params(("parallel", "parallel")),
        name="inproj",
    )(x, mods, nw, w)


def _delta_kernel(qz_ref, kz_ref, vz_ref, zz_ref, qx_ref, kx_ref, vx_ref, zx_ref,
                  gate_ref, cw_ref, par_ref, dnw_ref, ox_ref, oz_ref,
                  q_s, k_s, v_s, gf_s, gb_s, bf_s, bb_s, o_s, pad_s, kw_s, c_s, qe_s, o0_s, dc_s,
                  *, m_len, l_len):
    ltot = m_len + l_len
    n_ctx = m_len // CHUNK
    n_tot = ltot // CHUNK
    off = CHUNK

    def conv_silu(src_ref, w, length):
        pad_s[off - 8:off, :] = jnp.zeros((8, LANES), F32)
        pad_s[off:off + length, :] = src_ref[0]
        pad_s[off + length:off + length + 8, :] = jnp.zeros((8, LANES), F32)
        y = (w[0:1, :] * pad_s[off - 2:off - 2 + length, :]
             + w[1:2, :] * pad_s[off - 1:off - 1 + length, :]
             + w[2:3, :] * pad_s[off:off + length, :]
             + w[3:4, :] * pad_s[off + 1:off + 1 + length, :])
        return _silu(y)

    def l2norm(t):
        return t * lax.rsqrt(jnp.sum(t * t, axis=-1, keepdims=True) + L2_EPS)

    for (src_q, src_k, src_v, start, length) in ((qz_ref, kz_ref, vz_ref, 0, m_len),
                                                 (qx_ref, kx_ref, vx_ref, m_len, l_len)):
        q_s[start:start + length, :] = l2norm(conv_silu(src_q, cw_ref[0, 0], length)) * (DN_HEAD_DIM ** -0.5)
        k_s[start:start + length, :] = l2norm(conv_silu(src_k, cw_ref[0, 1], length))
        v_s[start:start + length, :] = conv_silu(src_v, cw_ref[0, 2], length)

    gt = gate_ref[0, 0]
    par = par_ref[0]
    shape = (ltot, LANES)
    def softplus(t):
        return jnp.maximum(t, 0.0) + jnp.log1p(jnp.exp(-jnp.abs(t)))

    beta = jax.nn.sigmoid(gt)
    a_log = jnp.concatenate([par[0:1, 0:2], par[0:1, 0:1], par[1:2, 0:1]], axis=1)
    dt_bias = jnp.concatenate([par[0:1, 0:2], par[2:3, 0:1], par[3:4, 0:1]], axis=1)
    log_decay = -jnp.exp(a_log) * softplus(gt + dt_bias)
    bf_s[...] = jnp.broadcast_to(beta[:, 0:1], shape)
    bb_s[...] = jnp.broadcast_to(beta[:, 1:2], shape)
    gf_s[...] = jnp.broadcast_to(log_decay[:, 2:3], shape)
    gb_s[...] = jnp.broadcast_to(log_decay[:, 3:4], shape)

    pos = lax.broadcasted_iota(jnp.int32, shape, 0) & (CHUNK - 1)
    pad_s[0:off, :] = jnp.zeros((off, LANES), F32)
    pad_s[off + ltot:off + ltot + off, :] = jnp.zeros((off, LANES), F32)
    step = 1
    while step < CHUNK:
        pad_s[off:off + ltot, :] = gf_s[...]
        gf_s[...] = gf_s[...] + jnp.where(pos >= step, pad_s[off - step:off - step + ltot, :], 0.0)
        pad_s[off:off + ltot, :] = gb_s[...]
        gb_s[...] = gb_s[...] + jnp.where(pos < CHUNK - step, pad_s[off + step:off + step + ltot, :], 0.0)
        step *= 2

    ri = lax.broadcasted_iota(jnp.int32, (CHUNK, CHUNK), 0)
    ci = lax.broadcasted_iota(jnp.int32, (CHUNK, CHUNK), 1)
    eye = (ri == ci).astype(F32)
    hd = DN_HEAD_DIM
    pair_mask = {1 << b: jnp.logical_and((ri >> (b + 1)) == (ci >> (b + 1)), (ri >> b) != (ci >> b))
                 for b in range(6)}

    group = max(g for g in range(1, 13) if n_tot % g == 0)

    def stage1(g, carry):
        chains = [(g * group + c, d) for c in range(group) for d in (0, 1)]
        both = lambda fn: [fn(n, d) for (n, d) in chains]
        each = lambda fn, *cols: [fn(*vals) for vals in zip(*cols)]
        rows = lambda n: pl.ds(pl.multiple_of(n * CHUNK, CHUNK), CHUNK)
        incl = both(lambda n, d: (ri >= ci) if d == 0 else (ri <= ci))
        strict = both(lambda n, d: (ri > ci) if d == 0 else (ri < ci))
        q = both(lambda n, d: q_s[rows(n), :])
        k = both(lambda n, d: k_s[rows(n), :])
        v = both(lambda n, d: v_s[rows(n), :])
        gi = both(lambda n, d: (gf_s if d == 0 else gb_s)[rows(n), :])
        bi = both(lambda n, d: (bf_s if d == 0 else bb_s)[rows(n), :])
        diff = each(lambda g_: g_[:, 0:CHUNK] - g_.T[0:CHUNK, :], gi)
        decay = each(lambda m, df: jnp.where(m, jnp.exp(jnp.where(m, df, 0.0)), 0.0), incl, diff)
        kb = each(lambda k_, b_: k_ * b_, k, bi)
        kk = each(_mm_nt, kb, k)
        low = each(lambda m, kk_, dec: jnp.where(m, kk_ * dec, 0.0), strict, kk, decay)
        inv = [eye - jnp.where(pair_mask[1], l_, 0.0) for l_ in low]
        size = 2
        while size < CHUNK:
            coupling = [jnp.where(pair_mask[size], l_, 0.0) for l_ in low]
            left = each(lambda a_, b_: _mm_inv(a_, b_), inv, coupling)
            corr = each(lambda a_, b_: _mm_inv(a_, b_), left, inv)
            inv = each(lambda t, u_: t - u_, inv, corr)
            size *= 2
        eg = [jnp.exp(g_) for g_ in gi]
        rhs = each(lambda kb_, e_, v_, b_: jnp.concatenate([kb_ * e_, v_ * b_], axis=1), kb, eg, v, bi)
        sol = each(_mm, inv, rhs)
        qk = each(_mm_nt, q, k)
        intra = each(lambda m, qk_, dec: jnp.where(m, qk_ * dec, 0.0), incl, qk, decay)
        g_last = [g_[CHUNK - 1:CHUNK, :] if d == 0 else g_[0:1, :] for g_, (_, d) in zip(gi, chains)]
        k_tail_t = each(lambda k_, gl, g_: (k_ * jnp.exp(gl - g_)).T, k, g_last, gi)
        iwu = each(_mm, intra, sol)
        kwu = each(_mm, k_tail_t, sol)
        for idx, (n, d) in enumerate(chains):
            sk = pl.ds(pl.multiple_of(n * hd, hd), hd)
            qe_s[d, rows(n), :] = (q[idx] * eg[idx] - iwu[idx][:, 0:hd]).astype(BF16)
            o0_s[d, rows(n), :] = iwu[idx][:, hd:]
            kw_s[d, sk, :] = kwu[idx][:, 0:hd].astype(BF16)
            c_s[d, sk, :] = kwu[idx][:, hd:]
            dc_s[d, pl.ds(pl.multiple_of(n * 8, 8), 8), :] = jnp.broadcast_to(jnp.exp(g_last[idx]), (8, LANES))
        return carry

    lax.fori_loop(0, n_tot // group, stage1, 0)

    def advance(direction, n, state):
        sl = pl.ds(pl.multiple_of(n * CHUNK, CHUNK), CHUNK)
        sk = pl.ds(pl.multiple_of(n * hd, hd), hd)
        sb = state.astype(BF16)
        o_s[direction, sl, :] = (jnp.dot(qe_s[direction, sl, :], sb, preferred_element_type=F32)
                                 + o0_s[direction, sl, :])
        dc = dc_s[direction, pl.ds(pl.multiple_of(n * 8, 8), 1), :]
        return (state * dc - jnp.dot(kw_s[direction, sk, :], sb, preferred_element_type=F32)
                + c_s[direction, sk, :])

    def stage2(t, states):
        n_b = jnp.where(t < n_ctx, n_ctx - 1 - t, n_tot + n_ctx - 1 - t)
        return advance(0, t, states[0]), advance(1, n_b, states[1])

    zero = jnp.zeros((hd, hd), F32)
    lax.fori_loop(0, n_tot, stage2, (zero, zero))

    o = o_s[0] + o_s[1]
    o = o * lax.rsqrt(jnp.mean(o * o, axis=-1, keepdims=True) + RMS_EPS) * dnw_ref[...]
    oz_ref[0] = o[0:m_len, :] * _silu(zz_ref[0])
    ox_ref[0] = o[m_len:, :] * _silu(zx_ref[0])


def _delta(pz, px, gates, conv_w, par, dn_norm_w):
    b, m_len, _ = pz.shape
    l_len = px.shape[1]
    ltot = m_len + l_len
    n_tot = ltot // CHUNK
    hb = LANES

    def col(c0, length):
        return pl.BlockSpec((1, length, hb), lambda i, h: (i, 0, c0 // hb + h))

    scratch = [pltpu.VMEM((ltot, LANES), F32) for _ in range(7)]
    scratch += [pltpu.VMEM((2, ltot, LANES), F32),
                pltpu.VMEM((ltot + 2 * CHUNK, LANES), F32),
                pltpu.VMEM((2, n_tot * DN_HEAD_DIM, LANES), BF16),
                pltpu.VMEM((2, n_tot * DN_HEAD_DIM, LANES), F32),
                pltpu.VMEM((2, ltot, LANES), BF16),
                pltpu.VMEM((2, ltot, LANES), F32),
                pltpu.VMEM((2, n_tot * 8, LANES), F32)]
    return pl.pallas_call(
        functools.partial(_delta_kernel, m_len=m_len, l_len=l_len),
        grid=(b, DN_HEADS),
        in_specs=[col(COL_Q, m_len), col(COL_K, m_len), col(COL_V, m_len), col(COL_Z, m_len),
                  col(COL_Q, l_len), col(COL_K, l_len), col(COL_V, l_len), col(COL_Z, l_len),
                  pl.BlockSpec((1, 1, ltot, 4), lambda i, h: (i, h, 0, 0)),
                  pl.BlockSpec((1, 3, CONV_WIDTH, hb), lambda i, h: (h, 0, 0, 0)),
                  pl.BlockSpec((1, 8, hb), lambda i, h: (h, 0, 0)),
                  pl.BlockSpec((1, hb), lambda i, h: (0, 0))],
        out_specs=[pl.BlockSpec((1, l_len, hb), lambda i, h: (i, 0, h)),
                   pl.BlockSpec((1, m_len, hb), lambda i, h: (i, 0, h))],
        out_shape=[jax.ShapeDtypeStruct((b, l_len, DN_WIDTH), F32),
                   jax.ShapeDtypeStruct((b, m_len, DN_WIDTH), F32)],
        scratch_shapes=scratch,
        compiler_params=_params(("parallel", "parallel")),
        name="delta",
    )(pz, pz, pz, pz, px, px, px, px, gates, conv_w, par, dn_norm_w)


def _fnet_weight_kernel(c_ref, s_ref, w_ref, o_ref, *, scale):
    w = w_ref[...]
    o_ref[:, 0:FNET_WIDTH] = _mm_f32(c_ref[...], w) * scale
    o_ref[:, FNET_WIDTH:] = _mm_f32(s_ref[...], w) * scale


def _fnet_weights(w_bd, seq_len):
    n = FNET_GROUP_DIM
    ang = 2.0 * np.pi * ((np.arange(n)[:, None] * np.arange(n)[None, :]) % n) / n
    eye4 = np.eye(FNET_WIDTH // n)
    c_bd = jnp.asarray(np.kron(eye4, np.cos(ang)), F32)
    s_bd = jnp.asarray(np.kron(eye4, np.sin(ang)), F32)
    return pl.pallas_call(
        functools.partial(_fnet_weight_kernel, scale=float(1.0 / math.sqrt(seq_len * n))),
        out_shape=jax.ShapeDtypeStruct((FNET_WIDTH, 2 * FNET_WIDTH), F32),
        name="fnet_weights",
    )(c_bd, s_bd, w_bd)


def _fnet_kernel(x_ref, w_ref, c_ref, s_ref, o_ref):
    xw = _mm(x_ref[0], w_ref[...])
    o_ref[0] = _mm(c_ref[...], xw[:, 0:FNET_WIDTH]) + _mm(s_ref[...], xw[:, FNET_WIDTH:])


def _dft_mats(seq_len):
    idx = (np.arange(seq_len)[:, None] * np.arange(seq_len)[None, :]) % seq_len
    ang = 2.0 * np.pi * idx / seq_len
    return jnp.asarray(np.cos(ang), F32).astype(BF16), jnp.asarray(-np.sin(ang), F32).astype(BF16)


def _fnet(p, wcs):
    b, l, _ = p.shape
    tm = min(512, l)
    cm, sm = _dft_mats(l)
    return pl.pallas_call(
        _fnet_kernel,
        grid=(l // tm, b),
        in_specs=[pl.BlockSpec((1, l, FNET_WIDTH), lambda i, j: (j, 0, COL_FNET // FNET_WIDTH)),
                  pl.BlockSpec((FNET_WIDTH, 2 * FNET_WIDTH), lambda i, j: (0, 0)),
                  pl.BlockSpec((tm, l), lambda i, j: (i, 0)),
                  pl.BlockSpec((tm, l), lambda i, j: (i, 0))],
        out_specs=pl.BlockSpec((1, tm, FNET_WIDTH), lambda i, j: (j, i, 0)),
        out_shape=jax.ShapeDtypeStruct((b, l, FNET_WIDTH), F32),
        compiler_params=_params(("parallel", "parallel")),
        name="fnet",
    )(p, wcs, cm, sm)


def _mixout_kernel(x_ref, dn_ref, p_ref, f_ref, band_ref, icnt_ref, wp_ref, ps_ref, wo_ref,
                   mod_ref, o_ref):
    xp = p_ref[0]
    group = lax.broadcasted_iota(jnp.int32, xp.shape, 1) // POOL_GROUP_DIM
    win_sum = jnp.zeros(xp.shape, F32)
    for g in range(len(POOL_WINDOWS)):
        win_sum = win_sum + _mm(band_ref[g], jnp.where(group == g, xp, 0.0))
    y = win_sum * icnt_ref[...] - xp
    pool = _mm(y, wp_ref[...]) * ps_ref[...]
    out = (_mm(dn_ref[0], wo_ref[0:DN_WIDTH, :])
           + _mm(pool, wo_ref[DN_WIDTH:DN_WIDTH + POOL_WIDTH, :])
           + _mm(f_ref[0], wo_ref[DN_WIDTH + POOL_WIDTH:, :]))
    o_ref[0] = x_ref[0] + mod_ref[0, 2:3, :] * out


def _pool_consts(tm, seg):
    pos = np.arange(tm) % seg
    seg_id = np.arange(tm) // seg
    band = np.zeros((len(POOL_WINDOWS), tm, tm), np.float32)
    icnt = np.zeros((tm, POOL_WIDTH), np.float32)
    for g, w in enumerate(POOL_WINDOWS):
        lo = np.clip(pos - w // 2, 0, seg)
        hi = np.clip(pos + w - w // 2, 0, seg)
        j = np.arange(tm)
        inside = (seg_id[:, None] == seg_id[None, :]) & (pos[None, :] >= lo[:, None]) & (pos[None, :] < hi[:, None])
        band[g] = inside.astype(np.float32)
        icnt[:, g * POOL_GROUP_DIM:(g + 1) * POOL_GROUP_DIM] = (1.0 / (hi - lo))[:, None]
    return jnp.asarray(band).astype(BF16), jnp.asarray(icnt)


def _mixout(x, dn, p, fn, seg, wp_bd, pool_scale, w_out, mods):
    b, l, d = x.shape
    tm = min(256, l)
    band, icnt = _pool_consts(tm, min(seg, tm))
    if seg > tm:
        raise ValueError("pooling segment longer than the token tile")
    return pl.pallas_call(
        _mixout_kernel,
        grid=(b, l // tm),
        in_specs=[pl.BlockSpec((1, tm, d), lambda i, j: (i, j, 0)),
                  pl.BlockSpec((1, tm, DN_WIDTH), lambda i, j: (i, j, 0)),
                  pl.BlockSpec((1, tm, POOL_WIDTH), lambda i, j: (i, j, COL_POOL // POOL_WIDTH)),
                  pl.BlockSpec((1, tm, FNET_WIDTH), lambda i, j: (i, j, 0)),
                  pl.BlockSpec((len(POOL_WINDOWS), tm, tm), lambda i, j: (0, 0, 0)),
                  pl.BlockSpec((tm, POOL_WIDTH), lambda i, j: (0, 0)),
                  pl.BlockSpec((POOL_WIDTH, POOL_WIDTH), lambda i, j: (0, 0)),
                  pl.BlockSpec((1, POOL_WIDTH), lambda i, j: (0, 0)),
                  pl.BlockSpec((d, d), lambda i, j: (0, 0)),
                  pl.BlockSpec((1, 8, d), lambda i, j: (i, 0, 0))],
        out_specs=pl.BlockSpec((1, tm, d), lambda i, j: (i, j, 0)),
        out_shape=jax.ShapeDtypeStruct((b, l, d), F32),
        compiler_params=_params(("parallel", "parallel")),
        name="mixout",
    )(x, dn, p, fn, band, icnt, wp_bd, pool_scale, w_out, mods)


def _peer_kernel(x_ref, mod_ref, nw_ref, fw_ref, wq_ref, keys_ref, u_ref, vt_ref, o_ref,
                 ht_s, c1_s, e1_s, r2_s, e2_s, acc_s, act_s, acta_s, wga_s, wgb_s, sc_s, rc_s, re_s,
                 *, n_chunks, blocks_per_half, final_norm):
    p = pl.program_id(2)
    neg_inf = float("-inf")
    half = blocks_per_half * N_KEYS
    n_lane_groups = x_ref.shape[1] // LANES

    @pl.when(p == 0)
    def _route():
        h = _norm_mod(x_ref[0], nw_ref[...], mod_ref[0, 3:4, :], mod_ref[0, 4:5, :])
        ht_s[...] = h.T.astype(BF16)

        marker = 2.0 ** 100

        def extract(s):
            vals = []
            for r in range(PEER_TOPK):
                mx = jnp.max(s, axis=0, keepdims=True)
                vals.append(mx)
                s = jnp.where(s == mx, -marker * (r + 1), s)
            rank = jnp.where(s < -0.5 * marker, s * (-1.0 / marker) - 1.0, 99.0)
            return vals, rank

        pairs = [(a, b) for a in range(PEER_TOPK) for b in range(PEER_TOPK)
                 if (a + 1) * (b + 1) <= PEER_TOPK]
        n_pad = (-len(pairs)) % 8

        for hd in range(PEER_HEADS):
            qt = jnp.dot(wq_ref[hd * 2 * N_KEYS:(hd + 1) * 2 * N_KEYS, :], ht_s[...],
                         preferred_element_type=F32)
            sc_s[0] = _mm(keys_ref[hd, 0], qt[0:N_KEYS, :])
            sc_s[1] = _mm(keys_ref[hd, 1], qt[N_KEYS:, :])
            for tg in range(n_lane_groups):
                ln = slice(tg * LANES, (tg + 1) * LANES)
                s1 = sc_s[0, :, ln]
                s2 = sc_s[1, :, ln]
                t1, rank1 = extract(s1)
                t2, rank2 = extract(s2)
                cands = [t1[a] + t2[b] for a, b in pairs]
                cand = jnp.concatenate(cands + [jnp.full_like(cands[0], neg_inf)] * n_pad, axis=0)
                work = cand
                thr = jnp.full_like(cands[0], neg_inf)
                found = jnp.zeros_like(cands[0])
                for _ in range(PEER_TOPK):
                    mx = jnp.max(work, axis=0, keepdims=True)
                    cnt = jnp.sum(jnp.where(cand >= mx, 1.0, 0.0), axis=0, keepdims=True)
                    hit = jnp.where(cnt >= PEER_TOPK, 1.0 - found, 0.0)
                    thr = jnp.where(hit > 0.0, mx, thr)
                    found = jnp.maximum(found, hit)
                    work = jnp.where(work == mx, neg_inf, work)
                z = jnp.sum(jnp.where(cand >= thr, jnp.exp(cand - cands[0]), 0.0), axis=0, keepdims=True)
                cnt1 = jnp.zeros(s1.shape, F32)
                for a in range(PEER_TOPK):
                    partners = jnp.zeros_like(thr)
                    for idx, (pa, _) in enumerate(pairs):
                        if pa == a:
                            partners = partners + jnp.where(cands[idx] >= thr, 1.0, 0.0)
                    cnt1 = jnp.where(rank1 == float(a), partners, cnt1)
                c1_s[hd, :, ln] = cnt1
                e1_s[hd, :, ln] = jnp.exp(s1 - t1[0]) * (1.0 / z)
                r2_s[hd, :, ln] = rank2.astype(BF16)
                e2_s[hd, :, ln] = jnp.exp(s2 - t2[0]).astype(BF16)
        acc_s[...] = jnp.zeros(acc_s.shape, F32)
        act_s[...] = jnp.zeros(act_s.shape, F32)
        wga_s[...] = jnp.zeros(wga_s.shape, BF16)

    def row_bf16(row):
        tile = jnp.broadcast_to(row, (16, LANES)).astype(BF16)
        return jnp.concatenate([tile] * (N_KEYS // 16), axis=0)

    def stage_rows(first_block, slot):
        for ii in range(blocks_per_half):
            i = jnp.maximum(first_block + ii, 0)
            for hd in range(PEER_HEADS):
                k = ii * PEER_HEADS + hd
                rc_s[slot, k:k + 1, :] = c1_s[hd, pl.ds(i, 1), :]
                re_s[slot, k:k + 1, :] = e1_s[hd, pl.ds(i, 1), :]

    def gate_unit(act, slot, out_ref, tg, g0):
        ln = slice(tg * LANES, (tg + 1) * LANES)
        blocks = list(range(g0, min(g0 + 4, blocks_per_half)))
        gates = [jnp.zeros((N_KEYS, LANES), BF16) for _ in blocks]
        for hd in range(PEER_HEADS):
            r2 = r2_s[hd, :, ln]
            e2 = e2_s[hd, :, ln]
            for n, ii in enumerate(blocks):
                k = ii * PEER_HEADS + hd
                c1row = row_bf16(rc_s[slot, k:k + 1, ln])
                e1row = row_bf16(re_s[slot, k:k + 1, ln])
                gates[n] = gates[n] + jnp.minimum(jnp.maximum(c1row - r2, 0.0), e1row) * e2
        for n, ii in enumerate(blocks):
            a = act[ii * N_KEYS:(ii + 1) * N_KEYS, ln]
            gelu = 0.5 * a * (1.0 + lax.erf(a * (2.0 ** -0.5)))
            out_ref[ii * N_KEYS:(ii + 1) * N_KEYS, ln] = gates[n] * gelu.astype(BF16)

    def gate_units(act, slot, out_ref, lane_groups):
        for tg in lane_groups:
            for g0 in range(0, blocks_per_half, 4):
                gate_unit(act, slot, out_ref, tg, g0)

    def value_mm(lo, w_ref):
        acc_s[...] += jnp.dot(vt_ref[:, lo:lo + half], w_ref[...], preferred_element_type=F32)

    def act_mm(lo, dst_ref):
        dst_ref[...] = jnp.dot(u_ref[lo:lo + half, :], ht_s[...], preferred_element_type=F32)

    prev_b = (2 * p - 1) * blocks_per_half
    first = tuple(range(n_lane_groups // 2))
    second = tuple(range(n_lane_groups // 2, n_lane_groups))

    @pl.when(p < n_chunks)
    def _steady():
        stage_rows(prev_b, 0)
        stage_rows(2 * p * blocks_per_half, 1)
        value_mm(0, wga_s)
        gate_units(act_s, 0, wgb_s, first)
        act_mm(0, acta_s)
        gate_units(act_s, 0, wgb_s, second)
        value_mm(half, wgb_s)
        gate_units(acta_s, 1, wga_s, first)
        act_mm(half, act_s)
        gate_units(acta_s, 1, wga_s, second)

    @pl.when(p == n_chunks)
    def _drain():
        stage_rows(prev_b, 0)
        value_mm(0, wga_s)
        gate_units(act_s, 0, wgb_s, first + second)
        out = acc_s[...] + jnp.dot(vt_ref[:, half:], wgb_s[...], preferred_element_type=F32)
        y = x_ref[0] + mod_ref[0, 5:6, :] * out.T
        if final_norm:
            y = y * lax.rsqrt(jnp.mean(y * y, axis=-1, keepdims=True) + RMS_EPS) * fw_ref[...]
        o_ref[0] = y


def _peer(x, mods, nw, wq_t, keys, u, v_t, layer, final_w, final_norm, tm=512, chunk=2048):
    b, l, d = x.shape
    tm = min(tm, l)
    n_chunks = N_EXPERTS // chunk
    half = chunk // 2
    kern = functools.partial(_peer_kernel, n_chunks=n_chunks, blocks_per_half=half // N_KEYS,
                             final_norm=final_norm)
    head_shape = (PEER_HEADS, N_KEYS, tm)
    once = pl.Buffered(1)
    return pl.pallas_call(
        kern,
        grid=(b, l // tm, n_chunks + 1),
        in_specs=[pl.BlockSpec((1, tm, d), lambda i, t, j: (i, t, 0)),
                  pl.BlockSpec((1, 8, d), lambda i, t, j: (i, 0, 0)),
                  pl.BlockSpec((1, d), lambda i, t, j: (0, 0)),
                  pl.BlockSpec((1, d), lambda i, t, j: (0, 0)),
                  pl.BlockSpec((None,) + wq_t.shape[1:], lambda i, t, j: (layer, 0, 0), pipeline_mode=once),
                  pl.BlockSpec((None,) + keys.shape[1:], lambda i, t, j: (layer, 0, 0, 0, 0),
                               pipeline_mode=once),
                  pl.BlockSpec((None, chunk, d), lambda i, t, j: (layer, jnp.minimum(j, n_chunks - 1), 0)),
                  pl.BlockSpec((None, d, chunk), lambda i, t, j: (layer, 0, jnp.maximum(j - 1, 0)))],
        out_specs=pl.BlockSpec((1, tm, d), lambda i, t, j: (i, t, 0)),
        out_shape=jax.ShapeDtypeStruct((b, l, d), F32),
        scratch_shapes=[pltpu.VMEM((d, tm), BF16),
                        pltpu.VMEM(head_shape, F32), pltpu.VMEM(head_shape, F32),
                        pltpu.VMEM(head_shape, BF16), pltpu.VMEM(head_shape, BF16),
                        pltpu.VMEM((d, tm), F32),
                        pltpu.VMEM((half, tm), F32), pltpu.VMEM((half, tm), F32),
                        pltpu.VMEM((half, tm), BF16), pltpu.VMEM((half, tm), BF16),
                        pltpu.VMEM((2, N_KEYS, tm), F32),
                        pltpu.VMEM((2, half // N_KEYS * PEER_HEADS, tm), F32),
                        pltpu.VMEM((2, half // N_KEYS * PEER_HEADS, tm), F32)],
        compiler_params=_params(("parallel", "parallel", "arbitrary")),
        name="peer",
    )(x, mods, nw, final_w, wq_t, keys, u, v_t)


def _block_diag(w):
    g, n, _ = w.shape
    out = jnp.zeros((g * n, g * n), w.dtype)
    for i in range(g):
        out = out.at[i * n:(i + 1) * n, i * n:(i + 1) * n].set(w[i])
    return out


def _reorder_w_in(w):
    d = w.shape[0]
    return jnp.concatenate([w[:, :REF_QKV], w[:, REF_ALPHA:REF_Z], w[:, REF_Z:REF_POOL],
                            w[:, REF_POOL:REF_FNET], w[:, REF_QKV:REF_ALPHA],
                            jnp.zeros((d, IN_PAD - REF_FNET), w.dtype)], axis=1).astype(BF16)


def _gate_columns(gz, gx):
    ba = jnp.concatenate([gz[:, :, 0:16], gx[:, :, 0:16]], axis=1)
    b, ltot, _ = ba.shape
    return ba.reshape(b, ltot, 4, DN_HEADS).transpose(0, 3, 1, 2)


def _mod_rows(mod, rows):
    d = mod.shape[1] // N_MOD
    m = mod.reshape(16, N_MOD, d)[rows]
    return jnp.concatenate([m, jnp.zeros((m.shape[0], 8 - N_MOD, d), m.dtype)], axis=1)


def kernel(x, c, ctx, c_ctx, w_mod, b_mod, norm1_w, norm2_w, w_in, conv_w, a_log, dt_bias,
           dn_norm_w, w_pool, pool_scale, w_fnet, w_out, w_query, sub_keys, expert_u, expert_v,
           final_norm_w):
    b, l, d = x.shape
    m = ctx.shape[1]
    depth = w_mod.shape[0]
    rows = l // GRID_W

    cs = jnp.concatenate([c, c_ctx[None, :], jnp.zeros((16 - b - 1, d), F32)], axis=0)
    mod_all = _modulation(cs, w_mod, b_mod)

    wq_t = jnp.swapaxes(w_query, 1, 2).astype(BF16)
    keys = sub_keys.astype(BF16)
    u_b = expert_u.astype(BF16)
    v_t = jnp.swapaxes(expert_v, 1, 2).astype(BF16)
    fw = final_norm_w[None, :]

    z = ctx
    for i in range(depth):
        update_ctx = i < depth - 1
        mods_x = _mod_rows(mod_all[i], jnp.arange(b))
        mods_z = _mod_rows(mod_all[i], jnp.full((b,), b))
        w_in_r = _reorder_w_in(w_in[i])
        nw1 = norm1_w[i][None, :]
        nw2 = norm2_w[i][None, :]

        px, gx = _inproj(x, mods_x, nw1, w_in_r)
        pz, gz = _inproj(z, mods_z, nw1, w_in_r)

        cw = conv_w[i].reshape(CONV_WIDTH, 3, DN_HEADS, DN_HEAD_DIM).transpose(2, 1, 0, 3)
        par = jnp.concatenate([a_log[i], dt_bias[i]], axis=0)
        par = jnp.broadcast_to(par.T[:, :, None], (DN_HEADS, 4, LANES))
        par = jnp.concatenate([par, jnp.zeros((DN_HEADS, 4, LANES), F32)], axis=1)
        dn_x, dn_z = _delta(pz, px, _gate_columns(gz, gx), cw, par, dn_norm_w[i][None, :])

        wp_bd = _block_diag(w_pool[i]).astype(BF16)
        wf_bd = _block_diag(w_fnet[i])
        w_out_b = w_out[i].astype(BF16)
        ps = pool_scale[i][None, :]

        fn_x = _fnet(px, _fnet_weights(wf_bd, l).astype(BF16))
        x = _mixout(x, dn_x, px, fn_x, l // rows, wp_bd, ps, w_out_b, mods_x)
        if update_ctx:
            fn_z = _fnet(pz, _fnet_weights(wf_bd, m).astype(BF16))
            z = _mixout(z, dn_z, pz, fn_z, m, wp_bd, ps, w_out_b, mods_z)

        x = _peer(x, mods_x, nw2, wq_t, keys, u_b, v_t, i, fw, not update_ctx)
        if update_ctx:
            z = _peer(z, mods_z, nw2, wq_t, keys, u_b, v_t, i, fw, False)

    return x
```

```python
import functools
import math

import numpy as np
import jax
import jax.numpy as jnp
from jax import lax
from jax.experimental import pallas as pl
from jax.experimental.pallas import tpu as pltpu

F32 = jnp.float32
BF16 = jnp.bfloat16
HIGHEST = lax.Precision.HIGHEST

D_MODEL = 1024
DN_HEADS = 4
DN_HEAD_DIM = 128
DN_WIDTH = DN_HEADS * DN_HEAD_DIM
POOL_WINDOWS = (2, 4, 8, 16)
POOL_WIDTH = 256
POOL_GROUP_DIM = 64
FNET_WIDTH = 256
FNET_GROUP_DIM = 64
CONV_WIDTH = 4
CHUNK = 64
GRID_W = 64
PEER_HEADS = 8
N_KEYS = 128
N_EXPERTS = N_KEYS * N_KEYS
PEER_TOPK = 16
N_MOD = 6
RMS_EPS = 1e-6
L2_EPS = 1e-6

COL_Q, COL_K, COL_V, COL_Z = 0, 512, 1024, 1536
COL_POOL, COL_FNET, COL_BA = 2048, 2304, 2560
IN_PAD = 2688
REF_QKV, REF_BETA, REF_ALPHA, REF_Z, REF_POOL, REF_FNET = 1536, 1544, 1552, 2064, 2320, 2576

LANES = 128
VMEM_LIMIT = 56 * 1024 * 1024


def _params(sem, flags=None):
    return pltpu.CompilerParams(dimension_semantics=sem, vmem_limit_bytes=VMEM_LIMIT, flags=flags)


def _mm(a, b):
    return jnp.dot(a.astype(BF16), b.astype(BF16), preferred_element_type=F32)


def _mm_nt(a, b):
    return lax.dot_general(a.astype(BF16), b.astype(BF16), (((1,), (1,)), ((), ())),
                           preferred_element_type=F32)


def _mm_inv(a, b):
    return _mm(a, b)


def _mm_f32(a, b):
    return jnp.dot(a, b, precision=HIGHEST, preferred_element_type=F32)


def _silu(x):
    return x * jax.nn.sigmoid(x)


def _mod_kernel(c_ref, w_ref, b_ref, o_ref):
    o_ref[0] = _mm_f32(_silu(c_ref[...]), w_ref[0]) + b_ref[0]


def _modulation(cs, w_mod, b_mod):
    depth, d, n = w_mod.shape
    tn = 1536
    return pl.pallas_call(
        _mod_kernel,
        grid=(depth, n // tn),
        in_specs=[pl.BlockSpec((16, d), lambda i, j: (0, 0)),
                  pl.BlockSpec((1, d, tn), lambda i, j: (i, 0, j)),
                  pl.BlockSpec((1, 1, tn), lambda i, j: (i, 0, j))],
        out_specs=pl.BlockSpec((1, 16, tn), lambda i, j: (i, 0, j)),
        out_shape=jax.ShapeDtypeStruct((depth, 16, n), F32),
        compiler_params=_params(("parallel", "parallel")),
        name="modulation",
    )(cs, w_mod, b_mod.reshape(depth, 1, n))


def _norm_mod(x, nw, shift, scale):
    y = x * lax.rsqrt(jnp.mean(x * x, axis=-1, keepdims=True) + RMS_EPS) * nw
    return y * (1.0 + scale) + shift


def _inproj_kernel(x_ref, mod_ref, nw_ref, w_ref, o_ref, g_ref):
    h = _norm_mod(x_ref[0], nw_ref[...], mod_ref[0, 0:1, :], mod_ref[0, 1:2, :])
    p = _mm(h, w_ref[...])
    o_ref[0] = p
    g_ref[0] = p[:, COL_BA:]


def _inproj(x, mods, nw, w):
    b, l, d = x.shape
    n = w.shape[1]
    tm = min(512, l)
    return pl.pallas_call(
        _inproj_kernel,
        grid=(b, l // tm),
        in_specs=[pl.BlockSpec((1, tm, d), lambda i, j: (i, j, 0)),
                  pl.BlockSpec((1, 8, d), lambda i, j: (i, 0, 0)),
                  pl.BlockSpec((1, d), lambda i, j: (0, 0)),
                  pl.BlockSpec((d, n), lambda i, j: (0, 0))],
        out_specs=[pl.BlockSpec((1, tm, n), lambda i, j: (i, j, 0)),
                   pl.BlockSpec((1, tm, n - COL_BA), lambda i, j: (i, j, 0))],
        out_shape=[jax.ShapeDtypeStruct((b, l, n), F32),
                   jax.ShapeDtypeStruct((b, l, n - COL_BA), F32)],
        compiler_params=_params(("parallel", "parallel")),
        name="inproj",
    )(x, mods, nw, w)


def _delta_kernel(qz_ref, kz_ref, vz_ref, zz_ref, qx_ref, kx_ref, vx_ref, zx_ref,
                  gate_ref, cw_ref, par_ref, dnw_ref, ox_ref, oz_ref,
                  q_s, k_s, v_s, gf_s, gb_s, bf_s, bb_s, o_s, pad_s, kw_s, c_s, qe_s, o0_s, dc_s,
                  *, m_len, l_len):
    ltot = m_len + l_len
    n_ctx = m_len // CHUNK
    n_tot = ltot // CHUNK
    off = CHUNK

    def conv_silu(src_ref, w, length):
        pad_s[off - 8:off, :] = jnp.zeros((8, LANES), F32)
        pad_s[off:off + length, :] = src_ref[0]
        pad_s[off + length:off + length + 8, :] = jnp.zeros((8, LANES), F32)
        y = (w[0:1, :] * pad_s[off - 2:off - 2 + length, :]
             + w[1:2, :] * pad_s[off - 1:off - 1 + length, :]
             + w[2:3, :] * pad_s[off:off + length, :]
             + w[3:4, :] * pad_s[off + 1:off + 1 + length, :])
        return _silu(y)

    def l2norm(t):
        return t * lax.rsqrt(jnp.sum(t * t, axis=-1, keepdims=True) + L2_EPS)

    for (src_q, src_k, src_v, start, length) in ((qz_ref, kz_ref, vz_ref, 0, m_len),
                                                 (qx_ref, kx_ref, vx_ref, m_len, l_len)):
        q_s[start:start + length, :] = l2norm(conv_silu(src_q, cw_ref[0, 0], length)) * (DN_HEAD_DIM ** -0.5)
        k_s[start:start + length, :] = l2norm(conv_silu(src_k, cw_ref[0, 1], length))
        v_s[start:start + length, :] = conv_silu(src_v, cw_ref[0, 2], length)

    gt = gate_ref[0, 0]
    par = par_ref[0]
    shape = (ltot, LANES)
    def softplus(t):
        return jnp.maximum(t, 0.0) + jnp.log1p(jnp.exp(-jnp.abs(t)))

    beta = jax.nn.sigmoid(gt)
    a_log = jnp.concatenate([par[0:1, 0:2], par[0:1, 0:1], par[1:2, 0:1]], axis=1)
    dt_bias = jnp.concatenate([par[0:1, 0:2], par[2:3, 0:1], par[3:4, 0:1]], axis=1)
    log_decay = -jnp.exp(a_log) * softplus(gt + dt_bias)
    bf_s[...] = jnp.broadcast_to(beta[:, 0:1], shape)
    bb_s[...] = jnp.broadcast_to(beta[:, 1:2], shape)
    gf_s[...] = jnp.broadcast_to(log_decay[:, 2:3], shape)
    gb_s[...] = jnp.broadcast_to(log_decay[:, 3:4], shape)

    pos = lax.broadcasted_iota(jnp.int32, shape, 0) & (CHUNK - 1)
    pad_s[0:off, :] = jnp.zeros((off, LANES), F32)
    pad_s[off + ltot:off + ltot + off, :] = jnp.zeros((off, LANES), F32)
    step = 1
    while step < CHUNK:
        pad_s[off:off + ltot, :] = gf_s[...]
        gf_s[...] = gf_s[...] + jnp.where(pos >= step, pad_s[off - step:off - step + ltot, :], 0.0)
        pad_s[off:off + ltot, :] = gb_s[...]
        gb_s[...] = gb_s[...] + jnp.where(pos < CHUNK - step, pad_s[off + step:off + step + ltot, :], 0.0)
        step *= 2

    ri = lax.broadcasted_iota(jnp.int32, (CHUNK, CHUNK), 0)
    ci = lax.broadcasted_iota(jnp.int32, (CHUNK, CHUNK), 1)
    eye = (ri == ci).astype(F32)
    hd = DN_HEAD_DIM
    pair_mask = {1 << b: jnp.logical_and((ri >> (b + 1)) == (ci >> (b + 1)), (ri >> b) != (ci >> b))
                 for b in range(6)}

    group = max(g for g in range(1, 13) if n_tot % g == 0)

    def stage1(g, carry):
        chains = [(g * group + c, d) for c in range(group) for d in (0, 1)]
        both = lambda fn: [fn(n, d) for (n, d) in chains]
        each = lambda fn, *cols: [fn(*vals) for vals in zip(*cols)]
        rows = lambda n: pl.ds(pl.multiple_of(n * CHUNK, CHUNK), CHUNK)
        incl = both(lambda n, d: (ri >= ci) if d == 0 else (ri <= ci))
        strict = both(lambda n, d: (ri > ci) if d == 0 else (ri < ci))
        q = both(lambda n, d: q_s[rows(n), :])
        k = both(lambda n, d: k_s[rows(n), :])
        v = both(lambda n, d: v_s[rows(n), :])
        gi = both(lambda n, d: (gf_s if d == 0 else gb_s)[rows(n), :])
        bi = both(lambda n, d: (bf_s if d == 0 else bb_s)[rows(n), :])
        diff = each(lambda g_: g_[:, 0:CHUNK] - g_.T[0:CHUNK, :], gi)
        decay = each(lambda m, df: jnp.where(m, jnp.exp(jnp.where(m, df, 0.0)), 0.0), incl, diff)
        kb = each(lambda k_, b_: k_ * b_, k, bi)
        kk = each(_mm_nt, kb, k)
        low = each(lambda m, kk_, dec: jnp.where(m, kk_ * dec, 0.0), strict, kk, decay)
        inv = [eye - jnp.where(pair_mask[1], l_, 0.0) for l_ in low]
        size = 2
        while size < CHUNK:
            coupling = [jnp.where(pair_mask[size], l_, 0.0) for l_ in low]
            left = each(lambda a_, b_: _mm_inv(a_, b_), inv, coupling)
            corr = each(lambda a_, b_: _mm_inv(a_, b_), left, inv)
            inv = each(lambda t, u_: t - u_, inv, corr)
            size *= 2
        eg = [jnp.exp(g_) for g_ in gi]
        rhs = each(lambda kb_, e_, v_, b_: jnp.concatenate([kb_ * e_, v_ * b_], axis=1), kb, eg, v, bi)
        sol = each(_mm, inv, rhs)
        qk = each(_mm_nt, q, k)
        intra = each(lambda m, qk_, dec: jnp.where(m, qk_ * dec, 0.0), incl, qk, decay)
        g_last = [g_[CHUNK - 1:CHUNK, :] if d == 0 else g_[0:1, :] for g_, (_, d) in zip(gi, chains)]
        k_tail_t = each(lambda k_, gl, g_: (k_ * jnp.exp(gl - g_)).T, k, g_last, gi)
        iwu = each(_mm, intra, sol)
        kwu = each(_mm, k_tail_t, sol)
        for idx, (n, d) in enumerate(chains):
            sk = pl.ds(pl.multiple_of(n * hd, hd), hd)
            qe_s[d, rows(n), :] = (q[idx] * eg[idx] - iwu[idx][:, 0:hd]).astype(BF16)
            o0_s[d, rows(n), :] = iwu[idx][:, hd:]
            kw_s[d, sk, :] = kwu[idx][:, 0:hd].astype(BF16)
            c_s[d, sk, :] = kwu[idx][:, hd:]
            dc_s[d, pl.ds(pl.multiple_of(n * 8, 8), 8), :] = jnp.broadcast_to(jnp.exp(g_last[idx]), (8, LANES))
        return carry

    lax.fori_loop(0, n_tot // group, stage1, 0)

    def advance(direction, n, state):
        sl = pl.ds(pl.multiple_of(n * CHUNK, CHUNK), CHUNK)
        sk = pl.ds(pl.multiple_of(n * hd, hd), hd)
        sb = state.astype(BF16)
        o_s[direction, sl, :] = (jnp.dot(qe_s[direction, sl, :], sb, preferred_element_type=F32)
                                 + o0_s[direction, sl, :])
        dc = dc_s[direction, pl.ds(pl.multiple_of(n * 8, 8), 1), :]
        return (state * dc - jnp.dot(kw_s[direction, sk, :], sb, preferred_element_type=F32)
                + c_s[direction, sk, :])

    def stage2(t, states):
        n_b = jnp.where(t < n_ctx, n_ctx - 1 - t, n_tot + n_ctx - 1 - t)
        return advance(0, t, states[0]), advance(1, n_b, states[1])

    zero = jnp.zeros((hd, hd), F32)
    lax.fori_loop(0, n_tot, stage2, (zero, zero))

    o = o_s[0] + o_s[1]
    o = o * lax.rsqrt(jnp.mean(o * o, axis=-1, keepdims=True) + RMS_EPS) * dnw_ref[...]
    oz_ref[0] = o[0:m_len, :] * _silu(zz_ref[0])
    ox_ref[0] = o[m_len:, :] * _silu(zx_ref[0])


def _delta(pz, px, gates, conv_w, par, dn_norm_w):
    b, m_len, _ = pz.shape
    l_len = px.shape[1]
    ltot = m_len + l_len
    n_tot = ltot // CHUNK
    hb = LANES

    def col(c0, length):
        return pl.BlockSpec((1, length, hb), lambda i, h: (i, 0, c0 // hb + h))

    scratch = [pltpu.VMEM((ltot, LANES), F32) for _ in range(7)]
    scratch += [pltpu.VMEM((2, ltot, LANES), F32),
                pltpu.VMEM((ltot + 2 * CHUNK, LANES), F32),
                pltpu.VMEM((2, n_tot * DN_HEAD_DIM, LANES), BF16),
                pltpu.VMEM((2, n_tot * DN_HEAD_DIM, LANES), F32),
                pltpu.VMEM((2, ltot, LANES), BF16),
                pltpu.VMEM((2, ltot, LANES), F32),
                pltpu.VMEM((2, n_tot * 8, LANES), F32)]
    return pl.pallas_call(
        functools.partial(_delta_kernel, m_len=m_len, l_len=l_len),
        grid=(b, DN_HEADS),
        in_specs=[col(COL_Q, m_len), col(COL_K, m_len), col(COL_V, m_len), col(COL_Z, m_len),
                  col(COL_Q, l_len), col(COL_K, l_len), col(COL_V, l_len), col(COL_Z, l_len),
                  pl.BlockSpec((1, 1, ltot, 4), lambda i, h: (i, h, 0, 0)),
                  pl.BlockSpec((1, 3, CONV_WIDTH, hb), lambda i, h: (h, 0, 0, 0)),
                  pl.BlockSpec((1, 8, hb), lambda i, h: (h, 0, 0)),
                  pl.BlockSpec((1, hb), lambda i, h: (0, 0))],
        out_specs=[pl.BlockSpec((1, l_len, hb), lambda i, h: (i, 0, h)),
                   pl.BlockSpec((1, m_len, hb), lambda i, h: (i, 0, h))],
        out_shape=[jax.ShapeDtypeStruct((b, l_len, DN_WIDTH), F32),
                   jax.ShapeDtypeStruct((b, m_len, DN_WIDTH), F32)],
        scratch_shapes=scratch,
        compiler_params=_params(("parallel", "parallel")),
        name="delta",
    )(pz, pz, pz, pz, px, px, px, px, gates, conv_w, par, dn_norm_w)


def _fnet_weight_kernel(c_ref, s_ref, w_ref, o_ref, *, scale):
    w = w_ref[...]
    o_ref[:, 0:FNET_WIDTH] = _mm_f32(c_ref[...], w) * scale
    o_ref[:, FNET_WIDTH:] = _mm_f32(s_ref[...], w) * scale


def _fnet_weights(w_bd, seq_len):
    n = FNET_GROUP_DIM
    ang = 2.0 * np.pi * ((np.arange(n)[:, None] * np.arange(n)[None, :]) % n) / n
    eye4 = np.eye(FNET_WIDTH // n)
    c_bd = jnp.asarray(np.kron(eye4, np.cos(ang)), F32)
    s_bd = jnp.asarray(np.kron(eye4, np.sin(ang)), F32)
    return pl.pallas_call(
        functools.partial(_fnet_weight_kernel, scale=float(1.0 / math.sqrt(seq_len * n))),
        out_shape=jax.ShapeDtypeStruct((FNET_WIDTH, 2 * FNET_WIDTH), F32),
        name="fnet_weights",
    )(c_bd, s_bd, w_bd)


def _fnet_kernel(x_ref, w_ref, c_ref, s_ref, o_ref):
    xw = _mm(x_ref[0], w_ref[...])
    o_ref[0] = _mm(c_ref[...], xw[:, 0:FNET_WIDTH]) + _mm(s_ref[...], xw[:, FNET_WIDTH:])


def _dft_mats(seq_len):
    idx = (np.arange(seq_len)[:, None] * np.arange(seq_len)[None, :]) % seq_len
    ang = 2.0 * np.pi * idx / seq_len
    return jnp.asarray(np.cos(ang), F32).astype(BF16), jnp.asarray(-np.sin(ang), F32).astype(BF16)


def _fnet(p, wcs):
    b, l, _ = p.shape
    tm = min(512, l)
    cm, sm = _dft_mats(l)
    return pl.pallas_call(
        _fnet_kernel,
        grid=(l // tm, b),
        in_specs=[pl.BlockSpec((1, l, FNET_WIDTH), lambda i, j: (j, 0, COL_FNET // FNET_WIDTH)),
                  pl.BlockSpec((FNET_WIDTH, 2 * FNET_WIDTH), lambda i, j: (0, 0)),
                  pl.BlockSpec((tm, l), lambda i, j: (i, 0)),
                  pl.BlockSpec((tm, l), lambda i, j: (i, 0))],
        out_specs=pl.BlockSpec((1, tm, FNET_WIDTH), lambda i, j: (j, i, 0)),
        out_shape=jax.ShapeDtypeStruct((b, l, FNET_WIDTH), F32),
        compiler_params=_params(("parallel", "parallel")),
        name="fnet",
    )(p, wcs, cm, sm)


def _mixout_kernel(x_ref, dn_ref, p_ref, f_ref, band_ref, icnt_ref, wp_ref, ps_ref, wo_ref,
                   mod_ref, o_ref):
    xp = p_ref[0]
    group = lax.broadcasted_iota(jnp.int32, xp.shape, 1) // POOL_GROUP_DIM
    win_sum = jnp.zeros(xp.shape, F32)
    for g in range(len(POOL_WINDOWS)):
        win_sum = win_sum + _mm(band_ref[g], jnp.where(group == g, xp, 0.0))
    y = win_sum * icnt_ref[...] - xp
    pool = _mm(y, wp_ref[...]) * ps_ref[...]
    out = (_mm(dn_ref[0], wo_ref[0:DN_WIDTH, :])
           + _mm(pool, wo_ref[DN_WIDTH:DN_WIDTH + POOL_WIDTH, :])
           + _mm(f_ref[0], wo_ref[DN_WIDTH + POOL_WIDTH:, :]))
    o_ref[0] = x_ref[0] + mod_ref[0, 2:3, :] * out


def _pool_consts(tm, seg):
    pos = np.arange(tm) % seg
    seg_id = np.arange(tm) // seg
    band = np.zeros((len(POOL_WINDOWS), tm, tm), np.float32)
    icnt = np.zeros((tm, POOL_WIDTH), np.float32)
    for g, w in enumerate(POOL_WINDOWS):
        lo = np.clip(pos - w // 2, 0, seg)
        hi = np.clip(pos + w - w // 2, 0, seg)
        j = np.arange(tm)
        inside = (seg_id[:, None] == seg_id[None, :]) & (pos[None, :] >= lo[:, None]) & (pos[None, :] < hi[:, None])
        band[g] = inside.astype(np.float32)
        icnt[:, g * POOL_GROUP_DIM:(g + 1) * POOL_GROUP_DIM] = (1.0 / (hi - lo))[:, None]
    return jnp.asarray(band).astype(BF16), jnp.asarray(icnt)


def _mixout(x, dn, p, fn, seg, wp_bd, pool_scale, w_out, mods):
    b, l, d = x.shape
    tm = min(256, l)
    band, icnt = _pool_consts(tm, min(seg, tm))
    if seg > tm:
        raise ValueError("pooling segment longer than the token tile")
    return pl.pallas_call(
        _mixout_kernel,
        grid=(b, l // tm),
        in_specs=[pl.BlockSpec((1, tm, d), lambda i, j: (i, j, 0)),
                  pl.BlockSpec((1, tm, DN_WIDTH), lambda i, j: (i, j, 0)),
                  pl.BlockSpec((1, tm, POOL_WIDTH), lambda i, j: (i, j, COL_POOL // POOL_WIDTH)),
                  pl.BlockSpec((1, tm, FNET_WIDTH), lambda i, j: (i, j, 0)),
                  pl.BlockSpec((len(POOL_WINDOWS), tm, tm), lambda i, j: (0, 0, 0)),
                  pl.BlockSpec((tm, POOL_WIDTH), lambda i, j: (0, 0)),
                  pl.BlockSpec((POOL_WIDTH, POOL_WIDTH), lambda i, j: (0, 0)),
                  pl.BlockSpec((1, POOL_WIDTH), lambda i, j: (0, 0)),
                  pl.BlockSpec((d, d), lambda i, j: (0, 0)),
                  pl.BlockSpec((1, 8, d), lambda i, j: (i, 0, 0))],
        out_specs=pl.BlockSpec((1, tm, d), lambda i, j: (i, j, 0)),
        out_shape=jax.ShapeDtypeStruct((b, l, d), F32),
        compiler_params=_params(("parallel", "parallel")),
        name="mixout",
    )(x, dn, p, fn, band, icnt, wp_bd, pool_scale, w_out, mods)


def _peer_kernel(x_ref, mod_ref, nw_ref, fw_ref, wq_ref, keys_ref, u_ref, vt_ref, o_ref,
                 ht_s, c1_s, e1_s, r2_s, e2_s, acc_s, act_s, acta_s, wga_s, wgb_s, sc_s, rc_s, re_s,
                 *, n_chunks, blocks_per_half, final_norm):
    p = pl.program_id(2)
    neg_inf = float("-inf")
    half = blocks_per_half * N_KEYS
    n_lane_groups = x_ref.shape[1] // LANES

    @pl.when(p == 0)
    def _route():
        h = _norm_mod(x_ref[0], nw_ref[...], mod_ref[0, 3:4, :], mod_ref[0, 4:5, :])
        ht_s[...] = h.T.astype(BF16)

        marker = 2.0 ** 100

        def extract(s):
            vals = []
            for r in range(PEER_TOPK):
                mx = jnp.max(s, axis=0, keepdims=True)
                vals.append(mx)
                s = jnp.where(s == mx, -marker * (r + 1), s)
            rank = jnp.where(s < -0.5 * marker, s * (-1.0 / marker) - 1.0, 99.0)
            return vals, rank

        pairs = [(a, b) for a in range(PEER_TOPK) for b in range(PEER_TOPK)
                 if (a + 1) * (b + 1) <= PEER_TOPK]
        n_pad = (-len(pairs)) % 8

        for hd in range(PEER_HEADS):
            qt = jnp.dot(wq_ref[hd * 2 * N_KEYS:(hd + 1) * 2 * N_KEYS, :], ht_s[...],
                         preferred_element_type=F32)
            sc_s[0] = _mm(keys_ref[hd, 0], qt[0:N_KEYS, :])
            sc_s[1] = _mm(keys_ref[hd, 1], qt[N_KEYS:, :])
            for tg in range(n_lane_groups):
                ln = slice(tg * LANES, (tg + 1) * LANES)
                s1 = sc_s[0, :, ln]
                s2 = sc_s[1, :, ln]
                t1, rank1 = extract(s1)
                t2, rank2 = extract(s2)
                cands = [t1[a] + t2[b] for a, b in pairs]
                cand = jnp.concatenate(cands + [jnp.full_like(cands[0], neg_inf)] * n_pad, axis=0)
                work = cand
                thr = jnp.full_like(cands[0], neg_inf)
                found = jnp.zeros_like(cands[0])
                for _ in range(PEER_TOPK):
                    mx = jnp.max(work, axis=0, keepdims=True)
                    cnt = jnp.sum(jnp.where(cand >= mx, 1.0, 0.0), axis=0, keepdims=True)
                    hit = jnp.where(cnt >= PEER_TOPK, 1.0 - found, 0.0)
                    thr = jnp.where(hit > 0.0, mx, thr)
                    found = jnp.maximum(found, hit)
                    work = jnp.where(work == mx, neg_inf, work)
                z = jnp.sum(jnp.where(cand >= thr, jnp.exp(cand - cands[0]), 0.0), axis=0, keepdims=True)
                cnt1 = jnp.zeros(s1.shape, F32)
                for a in range(PEER_TOPK):
                    partners = jnp.zeros_like(thr)
                    for idx, (pa, _) in enumerate(pairs):
                        if pa == a:
                            partners = partners + jnp.where(cands[idx] >= thr, 1.0, 0.0)
                    cnt1 = jnp.where(rank1 == float(a), partners, cnt1)
                c1_s[hd, :, ln] = cnt1
                e1_s[hd, :, ln] = jnp.exp(s1 - t1[0]) * (1.0 / z)
                r2_s[hd, :, ln] = rank2.astype(BF16)
                e2_s[hd, :, ln] = jnp.exp(s2 - t2[0]).astype(BF16)
        acc_s[...] = jnp.zeros(acc_s.shape, F32)
        act_s[...] = jnp.zeros(act_s.shape, F32)
        wga_s[...] = jnp.zeros(wga_s.shape, BF16)

    def row_bf16(row):
        tile = jnp.broadcast_to(row, (16, LANES)).astype(BF16)
        return jnp.concatenate([tile] * (N_KEYS // 16), axis=0)

    def stage_rows(first_block, slot):
        for ii in range(blocks_per_half):
            i = jnp.maximum(first_block + ii, 0)
            for hd in range(PEER_HEADS):
                k = ii * PEER_HEADS + hd
                rc_s[slot, k:k + 1, :] = c1_s[hd, pl.ds(i, 1), :]
                re_s[slot, k:k + 1, :] = e1_s[hd, pl.ds(i, 1), :]

    def gate_unit(act, slot, out_ref, tg, g0):
        ln = slice(tg * LANES, (tg + 1) * LANES)
        blocks = list(range(g0, min(g0 + 4, blocks_per_half)))
        gates = [jnp.zeros((N_KEYS, LANES), BF16) for _ in blocks]
        for hd in range(PEER_HEADS):
            r2 = r2_s[hd, :, ln]
            e2 = e2_s[hd, :, ln]
            for n, ii in enumerate(blocks):
                k = ii * PEER_HEADS + hd
                c1row = row_bf16(rc_s[slot, k:k + 1, ln])
                e1row = row_bf16(re_s[slot, k:k + 1, ln])
                gates[n] = gates[n] + jnp.minimum(jnp.maximum(c1row - r2, 0.0), e1row) * e2
        for n, ii in enumerate(blocks):
            a = act[ii * N_KEYS:(ii + 1) * N_KEYS, ln]
            gelu = 0.5 * a * (1.0 + lax.erf(a * (2.0 ** -0.5)))
            out_ref[ii * N_KEYS:(ii + 1) * N_KEYS, ln] = gates[n] * gelu.astype(BF16)

    def gate_units(act, slot, out_ref, lane_groups):
        for tg in lane_groups:
            for g0 in range(0, blocks_per_half, 4):
                gate_unit(act, slot, out_ref, tg, g0)

    def value_mm(lo, w_ref):
        acc_s[...] += jnp.dot(vt_ref[:, lo:lo + half], w_ref[...], preferred_element_type=F32)

    def act_mm(lo, dst_ref):
        dst_ref[...] = jnp.dot(u_ref[lo:lo + half, :], ht_s[...], preferred_element_type=F32)

    prev_b = (2 * p - 1) * blocks_per_half
    first = tuple(range(n_lane_groups // 2))
    second = tuple(range(n_lane_groups // 2, n_lane_groups))

    @pl.when(p < n_chunks)
    def _steady():
        stage_rows(prev_b, 0)
        stage_rows(2 * p * blocks_per_half, 1)
        value_mm(0, wga_s)
        gate_units(act_s, 0, wgb_s, first)
        act_mm(0, acta_s)
        gate_units(act_s, 0, wgb_s, second)
        value_mm(half, wgb_s)
        gate_units(acta_s, 1, wga_s, first)
        act_mm(half, act_s)
        gate_units(acta_s, 1, wga_s, second)

    @pl.when(p == n_chunks)
    def _drain():
        stage_rows(prev_b, 0)
        value_mm(0, wga_s)
        gate_units(act_s, 0, wgb_s, first + second)
        out = acc_s[...] + jnp.dot(vt_ref[:, half:], wgb_s[...], preferred_element_type=F32)
        y = x_ref[0] + mod_ref[0, 5:6, :] * out.T
        if final_norm:
            y = y * lax.rsqrt(jnp.mean(y * y, axis=-1, keepdims=True) + RMS_EPS) * fw_ref[...]
        o_ref[0] = y


def _peer(x, mods, nw, wq_t, keys, u, v_t, layer, final_w, final_norm, tm=512, chunk=2048):
    b, l, d = x.shape
    tm = min(tm, l)
    n_chunks = N_EXPERTS // chunk
    half = chunk // 2
    kern = functools.partial(_peer_kernel, n_chunks=n_chunks, blocks_per_half=half // N_KEYS,
                             final_norm=final_norm)
    head_shape = (PEER_HEADS, N_KEYS, tm)
    once = pl.Buffered(1)
    return pl.pallas_call(
        kern,
        grid=(b, l // tm, n_chunks + 1),
        in_specs=[pl.BlockSpec((1, tm, d), lambda i, t, j: (i, t, 0)),
                  pl.BlockSpec((1, 8, d), lambda i, t, j: (i, 0, 0)),
                  pl.BlockSpec((1, d), lambda i, t, j: (0, 0)),
                  pl.BlockSpec((1, d), lambda i, t, j: (0, 0)),
                  pl.BlockSpec((None,) + wq_t.shape[1:], lambda i, t, j: (layer, 0, 0), pipeline_mode=once),
                  pl.BlockSpec((None,) + keys.shape[1:], lambda i, t, j: (layer, 0, 0, 0, 0),
                               pipeline_mode=once),
                  pl.BlockSpec((None, chunk, d), lambda i, t, j: (layer, jnp.minimum(j, n_chunks - 1), 0)),
                  pl.BlockSpec((None, d, chunk), lambda i, t, j: (layer, 0, jnp.maximum(j - 1, 0)))],
        out_specs=pl.BlockSpec((1, tm, d), lambda i, t, j: (i, t, 0)),
        out_shape=jax.ShapeDtypeStruct((b, l, d), F32),
        scratch_shapes=[pltpu.VMEM((d, tm), BF16),
                        pltpu.VMEM(head_shape, F32), pltpu.VMEM(head_shape, F32),
                        pltpu.VMEM(head_shape, BF16), pltpu.VMEM(head_shape, BF16),
                        pltpu.VMEM((d, tm), F32),
                        pltpu.VMEM((half, tm), F32), pltpu.VMEM((half, tm), F32),
                        pltpu.VMEM((half, tm), BF16), pltpu.VMEM((half, tm), BF16),
                        pltpu.VMEM((2, N_KEYS, tm), F32),
                        pltpu.VMEM((2, half // N_KEYS * PEER_HEADS, tm), F32),
                        pltpu.VMEM((2, half // N_KEYS * PEER_HEADS, tm), F32)],
        compiler_params=_params(("parallel", "parallel", "arbitrary")),
        name="peer",
    )(x, mods, nw, final_w, wq_t, keys, u, v_t)


def _block_diag(w):
    g, n, _ = w.shape
    out = jnp.zeros((g * n, g * n), w.dtype)
    for i in range(g):
        out = out.at[i * n:(i + 1) * n, i * n:(i + 1) * n].set(w[i])
    return out


def _reorder_w_in(w):
    d = w.shape[0]
    return jnp.concatenate([w[:, :REF_QKV], w[:, REF_ALPHA:REF_Z], w[:, REF_Z:REF_POOL],
                            w[:, REF_POOL:REF_FNET], w[:, REF_QKV:REF_ALPHA],
                            jnp.zeros((d, IN_PAD - REF_FNET), w.dtype)], axis=1).astype(BF16)


def _gate_columns(gz, gx):
    ba = jnp.concatenate([gz[:, :, 0:16], gx[:, :, 0:16]], axis=1)
    b, ltot, _ = ba.shape
    return ba.reshape(b, ltot, 4, DN_HEADS).transpose(0, 3, 1, 2)


def _mod_rows(mod, rows):
    d = mod.shape[1] // N_MOD
    m = mod.reshape(16, N_MOD, d)[rows]
    return jnp.concatenate([m, jnp.zeros((m.shape[0], 8 - N_MOD, d), m.dtype)], axis=1)


def kernel(x, c, ctx, c_ctx, w_mod, b_mod, norm1_w, norm2_w, w_in, conv_w, a_log, dt_bias,
           dn_norm_w, w_pool, pool_scale, w_fnet, w_out, w_query, sub_keys, expert_u, expert_v,
           final_norm_w):
    b, l, d = x.shape
    m = ctx.shape[1]
    depth = w_mod.shape[0]
    rows = l // GRID_W

    cs = jnp.concatenate([c, c_ctx[None, :], jnp.zeros((16 - b - 1, d), F32)], axis=0)
    mod_all = _modulation(cs, w_mod, b_mod)

    wq_t = jnp.swapaxes(w_query, 1, 2).astype(BF16)
    keys = sub_keys.astype(BF16)
    u_b = expert_u.astype(BF16)
    v_t = jnp.swapaxes(expert_v, 1, 2).astype(BF16)
    fw = final_norm_w[None, :]

    z = ctx
    for i in range(depth):
        update_ctx = i < depth - 1
        mods_x = _mod_rows(mod_all[i], jnp.arange(b))
        mods_z = _mod_rows(mod_all[i], jnp.full((b,), b))
        w_in_r = _reorder_w_in(w_in[i])
        nw1 = norm1_w[i][None, :]
        nw2 = norm2_w[i][None, :]

        px, gx = _inproj(x, mods_x, nw1, w_in_r)
        pz, gz = _inproj(z, mods_z, nw1, w_in_r)

        cw = conv_w[i].reshape(CONV_WIDTH, 3, DN_HEADS, DN_HEAD_DIM).transpose(2, 1, 0, 3)
        par = jnp.concatenate([a_log[i], dt_bias[i]], axis=0)
        par = jnp.broadcast_to(par.T[:, :, None], (DN_HEADS, 4, LANES))
        par = jnp.concatenate([par, jnp.zeros((DN_HEADS, 4, LANES), F32)], axis=1)
        dn_x, dn_z = _delta(pz, px, _gate_columns(gz, gx), cw, par, dn_norm_w[i][None, :])

        wp_bd = _block_diag(w_pool[i]).astype(BF16)
        wf_bd = _block_diag(w_fnet[i])
        w_out_b = w_out[i].astype(BF16)
        ps = pool_scale[i][None, :]

        fn_x = _fnet(px, _fnet_weights(wf_bd, l).astype(BF16))
        x = _mixout(x, dn_x, px, fn_x, l // rows, wp_bd, ps, w_out_b, mods_x)
        if update_ctx:
            fn_z = _fnet(pz, _fnet_weights(wf_bd, m).astype(BF16))
            z = _mixout(z, dn_z, pz, fn_z, m, wp_bd, ps, w_out_b, mods_z)

        x = _peer(x, mods_x, nw2, wq_t, keys, u_b, v_t, i, fw, not update_ctx)
        if update_ctx:
            rows_z = 512 if (b * m) % 512 == 0 else m
            zt = z.reshape(b * m // rows_z, rows_z, d)
            mods_t = jnp.broadcast_to(mods_z[:1], (zt.shape[0],) + mods_z.shape[1:])
            zt = _peer(zt, mods_t, nw2, wq_t, keys, u_b, v_t, i, fw, False)
            z = zt.reshape(b, m, d)

    return x
```
